```python
import jax, jax.numpy as jnp
from jax import lax
import numpy as np

D_MODEL = 2048
BATCH = 32
SEQ = 256
DEPTH = 4
DEC_BATCH = 4
DEC_SEQ = 4096
PAST_LEN = 256

GRID_W = 64
N_MIXERS = 2
N_MLSTM_LAYERS = (DEPTH + 1) // 2
N_ATTN_LAYERS = DEPTH // 2
MLSTM_HEADS = 8
MLSTM_DV = D_MODEL // MLSTM_HEADS
MLSTM_DK = MLSTM_DV // 2
MLSTM_CHUNK = 128
FGATE_BIAS = 3.0
ATTN_HEADS = 16
ATTN_KV_HEADS = 4
ATTN_GROUP = ATTN_HEADS // ATTN_KV_HEADS
HEAD_DIM = D_MODEL // ATTN_HEADS
WINDOW = 128
ATTN_BLOCK = 128
ROPE_BASE = 10000.0
AXIS_DIM = HEAD_DIM // 2
N_EXPERTS = 16
CAPACITY_FACTOR = 2
D_FF_EXPERT = 2048
N_MOD = 6
EPS = 1e-6

kernel_name = 'hybrid_mlstm_swa_ec_diffusion_step'


def rmsnorm(x, g):
    xf = x.astype(jnp.float32)
    y = xf * lax.rsqrt(jnp.mean(xf * xf, axis=-1, keepdims=True) + EPS)
    return (y * g.astype(jnp.float32)).astype(x.dtype)


def adaln_params(cvec, w_mod_l, b_mod_l):
    return jnp.split(jax.nn.silu(cvec) @ w_mod_l + b_mod_l, N_MOD, axis=-1)


def modulate(h, shift, scale):
    return h * (1 + scale) + shift


def mlstm_chunkwise(q, k, v, log_i, log_f, c0, n0, m0):
    B, NH, T, DK = q.shape
    DV = v.shape[-1]
    L = MLSTM_CHUNK
    NC = T // L
    qc = q.reshape(B, NH, NC, L, DK)
    kc = k.reshape(B, NH, NC, L, DK)
    vc = v.reshape(B, NH, NC, L, DV)
    li = log_i.reshape(B, NH, NC, L)
    b = jnp.cumsum(log_f.reshape(B, NH, NC, L), axis=-1)
    b_last = b[..., -1]
    a = b_last[..., None] - b + li
    m_loc = jnp.max(a, axis=-1)
    w = jnp.exp(a - m_loc[..., None])
    c_loc = jnp.einsum('bhcsk,bhcsv->bhckv', kc * w[..., None], vc)
    n_loc = jnp.einsum('bhcs,bhcsk->bhck', w, kc)

    def step(carry, xs):
        c_prev, n_prev, m_prev = carry
        bl, ml, cl, nl = xs
        m_new = jnp.maximum(bl + m_prev, ml)
        w_old = jnp.exp(bl + m_prev - m_new)
        w_new = jnp.exp(ml - m_new)
        c_new = w_old[..., None, None] * c_prev + w_new[..., None, None] * cl
        n_new = w_old[..., None] * n_prev + w_new[..., None] * nl
        return (c_new, n_new, m_new), (c_prev, n_prev, m_prev)

    xs = (jnp.moveaxis(b_last, 2, 0), jnp.moveaxis(m_loc, 2, 0),
          jnp.moveaxis(c_loc, 2, 0), jnp.moveaxis(n_loc, 2, 0))
    init = (c0.astype(jnp.float32), n0.astype(jnp.float32), m0.astype(jnp.float32))
    final, starts = lax.scan(step, init, xs)
    c_s, n_s, m_s = [jnp.moveaxis(t, 0, 2) for t in starts]
    g = b + m_s[..., None]
    d = b[..., :, None] - b[..., None, :] + li[..., None, :]
    tri = jnp.tril(jnp.ones((L, L), dtype=bool))
    d = jnp.where(tri, d, -jnp.inf)
    m_t = jnp.maximum(g, jnp.max(d, axis=-1))
    w_inter = jnp.exp(g - m_t)
    w_intra = jnp.exp(d - m_t[..., None])
    s = jnp.einsum('bhctk,bhcsk->bhcts', qc, kc) * w_intra
    num = (w_inter[..., None] * jnp.einsum('bhctk,bhckv->bhctv', qc, c_s)
           + jnp.einsum('bhcts,bhcsv->bhctv', s, vc))
    den = w_inter * jnp.einsum('bhctk,bhck->bhct', qc, n_s) + jnp.sum(s, axis=-1)
    h = num / jnp.maximum(jnp.abs(den), jnp.exp(-m_t))[..., None]
    return h.reshape(B, NH, T, DV), final


def mlstm_mixer(h, w_in, b_gate, norm_g, w_out, init_f, init_b):
    B, T, _ = h.shape
    qk = MLSTM_HEADS * MLSTM_DK
    vw = MLSTM_HEADS * MLSTM_DV
    q, k, v, o, gp = jnp.split(h @ w_in, [qk, 2 * qk, 2 * qk + vw, 2 * qk + 2 * vw], axis=-1)

    def heads(t, dh):
        return t.reshape(B, T, MLSTM_HEADS, dh).transpose(0, 2, 1, 3).astype(jnp.float32)

    q = heads(q, MLSTM_DK) * MLSTM_DK ** -0.5
    k = heads(k, MLSTM_DK)
    v = heads(v, MLSTM_DV)
    g = (gp + b_gate).astype(jnp.float32).reshape(B, T, 4, MLSTM_HEADS).transpose(2, 0, 3, 1)
    log_i_f, log_f_f = g[0], jax.nn.log_sigmoid(g[1])
    log_i_b, log_f_b = g[2], jax.nn.log_sigmoid(g[3])
    h_f, st_f = mlstm_chunkwise(q, k, v, log_i_f, log_f_f, *init_f)

    def rev(t):
        return jnp.flip(t, axis=2)

    h_b, st_b = mlstm_chunkwise(rev(q), rev(k), rev(v), rev(log_i_b), rev(log_f_b), *init_b)
    hs = h_f + rev(h_b)
    hs = hs * lax.rsqrt(jnp.mean(hs * hs, axis=-1, keepdims=True) + EPS)
    hs = hs * norm_g.astype(jnp.float32).reshape(MLSTM_HEADS, MLSTM_DV)[None, :, None, :]
    hs = hs.transpose(0, 2, 1, 3).reshape(B, T, vw).astype(h.dtype)
    y = (hs * jax.nn.sigmoid(o)) @ w_out
    return y, st_f, st_b


def attn_qkv(h, w_in, qn_g, kn_g):
    B, T, _ = h.shape
    q, k, v = jnp.split(h @ w_in, [ATTN_HEADS * HEAD_DIM, (ATTN_HEADS + ATTN_KV_HEADS) * HEAD_DIM], axis=-1)
    q = rmsnorm(q.reshape(B, T, ATTN_HEADS, HEAD_DIM), qn_g).transpose(0, 2, 1, 3)
    k = rmsnorm(k.reshape(B, T, ATTN_KV_HEADS, HEAD_DIM), kn_g).transpose(0, 2, 1, 3)
    v = v.reshape(B, T, ATTN_KV_HEADS, HEAD_DIM).transpose(0, 2, 1, 3)
    return q, k, v


def axial_rope_angles(n):
    rows = n // GRID_W
    row = jnp.repeat(jnp.arange(rows, dtype=jnp.float32), GRID_W)
    col = (jnp.arange(n) % GRID_W).astype(jnp.float32)
    inv = ROPE_BASE ** (-jnp.arange(0, AXIS_DIM, 2, dtype=jnp.float32) / AXIS_DIM)
    return row[:, None] * inv, col[:, None] * inv


def rope_1d(x, ang):
    cos, sin = jnp.cos(ang), jnp.sin(ang)
    x1, x2 = jnp.split(x, 2, axis=-1)
    return jnp.concatenate([x1 * cos - x2 * sin, x2 * cos + x1 * sin], axis=-1)


def apply_axial_rope(x, ang_row, ang_col):
    xf = x.astype(jnp.float32)
    out = jnp.concatenate([rope_1d(xf[..., :AXIS_DIM], ang_row), rope_1d(xf[..., AXIS_DIM:], ang_col)], axis=-1)
    return out.astype(x.dtype)


def sink_softmax(s, sink):
    sk = jnp.broadcast_to(sink.astype(jnp.float32).reshape(ATTN_KV_HEADS, ATTN_GROUP, 1, 1), s.shape[:-1] + (1,))
    return jax.nn.softmax(jnp.concatenate([s, sk], axis=-1), axis=-1)[..., :-1]


def context_attention(q, k, v, sink):
    B, H, S, DH = q.shape
    nblk = S // ATTN_BLOCK
    qb = jnp.moveaxis(q.reshape(B, ATTN_KV_HEADS, ATTN_GROUP, nblk, ATTN_BLOCK, DH), 3, 0)

    def block(qi):
        s = jnp.einsum('bkgqd,bkjd->bkgqj', qi, k).astype(jnp.float32) * HEAD_DIM ** -0.5
        p = sink_softmax(s, sink).astype(v.dtype)
        return jnp.einsum('bkgqj,bkjd->bkgqd', p, v)

    o = lax.map(block, qb)
    return jnp.moveaxis(o, 0, 3).reshape(B, H, S, DH)


def latent_attention(q, k, v, k_ctx, v_ctx, sink):
    B, H, N, DH = q.shape
    nblk = N // ATTN_BLOCK
    span = ATTN_BLOCK + 2 * WINDOW
    qb = jnp.moveaxis(q.reshape(B, ATTN_KV_HEADS, ATTN_GROUP, nblk, ATTN_BLOCK, DH), 3, 0)
    pad = ((0, 0), (0, 0), (WINDOW, WINDOW), (0, 0))
    kp = jnp.pad(k, pad)
    vp = jnp.pad(v, pad)
    qpos = jnp.arange(ATTN_BLOCK)
    kpos = jnp.arange(span) - WINDOW
    in_band = jnp.abs(kpos[None, :] - qpos[:, None]) <= WINDOW
    scale = HEAD_DIM ** -0.5

    def block(args):
        i, qi = args
        start = i * ATTN_BLOCK
        kw = lax.dynamic_slice_in_dim(kp, start, span, axis=2)
        vw = lax.dynamic_slice_in_dim(vp, start, span, axis=2)
        kabs = start + kpos
        valid = in_band & ((kabs >= 0) & (kabs < N))[None, :]
        s_w = jnp.einsum('bkgqd,bkjd->bkgqj', qi, kw).astype(jnp.float32) * scale
        s_w = jnp.where(valid, s_w, -jnp.inf)
        s_c = jnp.einsum('bkgqd,bkjd->bkgqj', qi, k_ctx).astype(jnp.float32) * scale
        p = sink_softmax(jnp.concatenate([s_w, s_c], axis=-1), sink).astype(v.dtype)
        return (jnp.einsum('bkgqj,bkjd->bkgqd', p[..., :span], vw)
                + jnp.einsum('bkgqj,bkjd->bkgqd', p[..., span:], v_ctx))

    o = lax.map(block, (jnp.arange(nblk), qb))
    return jnp.moveaxis(o, 0, 3).reshape(B, H, N, DH)


def attn_out(o, w_out):
    B, H, T, DH = o.shape
    return o.transpose(0, 2, 1, 3).reshape(B, T, H * DH) @ w_out


def expert_choice_ffn(h, w_router, w_gate, w_up, w_down):
    B, N, D = h.shape
    cap = CAPACITY_FACTOR * N // N_EXPERTS
    aff = jax.nn.softmax((h @ w_router).astype(jnp.float32), axis=-1)
    gate, idx = lax.top_k(jnp.swapaxes(aff, 1, 2), cap)
    xg = jax.vmap(lambda hb, ib: hb[ib])(h, idx)
    a = jnp.einsum('becd,edf->becf', xg, w_gate)
    u = jnp.einsum('becd,edf->becf', xg, w_up)
    y = jnp.einsum('becf,efd->becd', jax.nn.silu(a) * u, w_down) * gate[..., None].astype(h.dtype)
    return jax.vmap(lambda yb, ib: jnp.zeros((N, D), h.dtype).at[ib.reshape(-1)].add(yb.reshape(-1, D)))(y, idx)


def setup_inputs(seed: int = 0) -> dict:
    key = jax.random.key(seed)
    ks = jax.random.split(key, 26)
    f32 = jnp.float32

    def nrm(k, shape, s):
        return jax.random.normal(k, shape, f32) * s

    D = D_MODEL
    m_cols = 2 * MLSTM_HEADS * MLSTM_DK + 2 * MLSTM_HEADS * MLSTM_DV + 4 * MLSTM_HEADS
    a_cols = (ATTN_HEADS + 2 * ATTN_KV_HEADS) * HEAD_DIM
    gate_offset = jnp.array([0.0, FGATE_BIAS, 0.0, FGATE_BIAS], f32)[None, :, None]
    m_b_gate = (gate_offset + nrm(ks[14], (N_MLSTM_LAYERS, 4, MLSTM_HEADS), 0.1)).reshape(N_MLSTM_LAYERS, 4 * MLSTM_HEADS)
    return {
        'x_prompt': nrm(ks[0], (BATCH, SEQ, D), 1.0),
        'x_sample': nrm(ks[1], (DEC_BATCH, DEC_SEQ, D), 1.0),
        'state_mlstm_C': nrm(ks[2], (DEC_BATCH, N_MLSTM_LAYERS, 2, MLSTM_HEADS, MLSTM_DK, MLSTM_DV), 0.1),
        'state_mlstm_n': nrm(ks[3], (DEC_BATCH, N_MLSTM_LAYERS, 2, MLSTM_HEADS, MLSTM_DK), 1.0),
        'state_mlstm_m': nrm(ks[4], (DEC_BATCH, N_MLSTM_LAYERS, 2, MLSTM_HEADS), 0.5),
        'cache_attn_k': nrm(ks[5], (DEC_BATCH, N_ATTN_LAYERS, ATTN_KV_HEADS, PAST_LEN, HEAD_DIM), 1.0),
        'cache_attn_v': nrm(ks[6], (DEC_BATCH, N_ATTN_LAYERS, ATTN_KV_HEADS, PAST_LEN, HEAD_DIM), 1.0),
        'c': nrm(ks[7], (DEC_BATCH, D), 1.0),
        'c_ctx': nrm(ks[8], (D,), 1.0),
        'w_mod': nrm(ks[9], (DEPTH, D, N_MOD * D), 0.5 * D ** -0.5),
        'b_mod': nrm(ks[10], (DEPTH, N_MOD * D), 0.01),
        'norm1_g': 1.0 + nrm(ks[11], (DEPTH, D), 0.1),
        'norm2_g': 1.0 + nrm(ks[12], (DEPTH, D), 0.1),
        'm_w_in': nrm(ks[13], (N_MLSTM_LAYERS, D, m_cols), D ** -0.5),
        'm_b_gate': m_b_gate,
        'm_norm_g': 1.0 + nrm(ks[15], (N_MLSTM_LAYERS, MLSTM_HEADS * MLSTM_DV), 0.1),
        'm_w_out': nrm(ks[16], (N_MLSTM_LAYERS, MLSTM_HEADS * MLSTM_DV, D), (MLSTM_HEADS * MLSTM_DV) ** -0.5),
        'a_w_in': nrm(ks[17], (N_ATTN_LAYERS, D, a_cols), D ** -0.5),
        'a_q_norm_g': 1.0 + nrm(ks[18], (N_ATTN_LAYERS, HEAD_DIM), 0.1),
        'a_k_norm_g': 1.0 + nrm(ks[19], (N_ATTN_LAYERS, HEAD_DIM), 0.1),
        'a_sink': nrm(ks[20], (N_ATTN_LAYERS, ATTN_HEADS), 0.5),
        'a_w_out': nrm(ks[21], (N_ATTN_LAYERS, ATTN_HEADS * HEAD_DIM, D), (ATTN_HEADS * HEAD_DIM) ** -0.5),
        'e_w_router': nrm(ks[22], (DEPTH, D, N_EXPERTS), D ** -0.5),
        'e_w_gate': nrm(ks[23], (DEPTH, N_EXPERTS, D, D_FF_EXPERT), D ** -0.5),
        'e_w_up': nrm(ks[24], (DEPTH, N_EXPERTS, D, D_FF_EXPERT), D ** -0.5),
        'e_w_down': nrm(ks[25], (DEPTH, N_EXPERTS, D_FF_EXPERT, D), D_FF_EXPERT ** -0.5),
    }


def reference(x_prompt, x_sample, state_mlstm_C, state_mlstm_n, state_mlstm_m, cache_attn_k, cache_attn_v,
              c, c_ctx, w_mod, b_mod, norm1_g, norm2_g, m_w_in, m_b_gate, m_norm_g, m_w_out,
              a_w_in, a_q_norm_g, a_k_norm_g, a_sink, a_w_out, e_w_router, e_w_gate, e_w_up, e_w_down):
    xp = x_prompt
    xs = x_sample
    Bp = xp.shape[0]
    N = xs.shape[1]
    ang_row, ang_col = axial_rope_angles(N)
    c_lat = c[:, None, :]
    new_C, new_n, new_m, new_k, new_v = [], [], [], [], []
    for l in range(DEPTH):
        j = l // N_MIXERS
        p_sh1, p_sc1, p_g1, p_sh2, p_sc2, p_g2 = adaln_params(c_ctx, w_mod[l], b_mod[l])
        s_sh1, s_sc1, s_g1, s_sh2, s_sc2, s_g2 = adaln_params(c_lat, w_mod[l], b_mod[l])
        hp = modulate(rmsnorm(xp, norm1_g[l]), p_sh1, p_sc1)
        hs = modulate(rmsnorm(xs, norm1_g[l]), s_sh1, s_sc1)
        if l % N_MIXERS == 0:
            zero = (jnp.zeros((Bp, MLSTM_HEADS, MLSTM_DK, MLSTM_DV), jnp.float32),
                    jnp.zeros((Bp, MLSTM_HEADS, MLSTM_DK), jnp.float32),
                    jnp.zeros((Bp, MLSTM_HEADS), jnp.float32))
            yp, st_f, st_b = mlstm_mixer(hp, m_w_in[j], m_b_gate[j], m_norm_g[j], m_w_out[j], zero, zero)
            init_f = (state_mlstm_C[:, j, 0], state_mlstm_n[:, j, 0], state_mlstm_m[:, j, 0])
            init_b = (state_mlstm_C[:, j, 1], state_mlstm_n[:, j, 1], state_mlstm_m[:, j, 1])
            ys, _, _ = mlstm_mixer(hs, m_w_in[j], m_b_gate[j], m_norm_g[j], m_w_out[j], init_f, init_b)
            new_C.append(jnp.stack([st_f[0], st_b[0]], axis=1))
            new_n.append(jnp.stack([st_f[1], st_b[1]], axis=1))
            new_m.append(jnp.stack([st_f[2], st_b[2]], axis=1))
        else:
            qp, kp, vp = attn_qkv(hp, a_w_in[j], a_q_norm_g[j], a_k_norm_g[j])
            yp = attn_out(context_attention(qp, kp, vp, a_sink[j]), a_w_out[j])
            qs, ks_, vs = attn_qkv(hs, a_w_in[j], a_q_norm_g[j], a_k_norm_g[j])
            qs = apply_axial_rope(qs, ang_row, ang_col)
            ks_ = apply_axial_rope(ks_, ang_row, ang_col)
            ys = attn_out(latent_attention(qs, ks_, vs, cache_attn_k[:, j], cache_attn_v[:, j], a_sink[j]), a_w_out[j])
            new_k.append(kp)
            new_v.append(vp)
        xp = xp + p_g1 * yp
        xs = xs + s_g1 * ys
        hp = modulate(rmsnorm(xp, norm2_g[l]), p_sh2, p_sc2)
        hs = modulate(rmsnorm(xs, norm2_g[l]), s_sh2, s_sc2)
        xp = xp + p_g2 * expert_choice_ffn(hp, e_w_router[l], e_w_gate[l], e_w_up[l], e_w_down[l])
        xs = xs + s_g2 * expert_choice_ffn(hs, e_w_router[l], e_w_gate[l], e_w_up[l], e_w_down[l])
    return (xp, xs, jnp.stack(new_C, axis=1), jnp.stack(new_n, axis=1), jnp.stack(new_m, axis=1),
            jnp.stack(new_k, axis=1), jnp.stack(new_v, axis=1))
```

```python
import functools

import jax
import jax.numpy as jnp
import numpy as np
from jax import lax
from jax.experimental import pallas as pl
from jax.experimental.pallas import tpu as pltpu

F32 = jnp.float32
BF16 = jnp.bfloat16

GRID_W = 64
MLSTM_HEADS = 8
ATTN_HEADS = 16
ATTN_KV_HEADS = 4
ATTN_GROUP = ATTN_HEADS // ATTN_KV_HEADS
WINDOW = 128
ROPE_BASE = 10000.0
N_EXPERTS = 16
CAPACITY_FACTOR = 2
N_MOD = 6
EPS = 1e-6
N_GROUPS_PAD = 8
LANES = 128
MLSTM_CHUNK = 256
ATTN_BLOCK = 128
VMEM_LIMIT = 56 * 1024 * 1024


def _cparams(sem):
    return pltpu.CompilerParams(dimension_semantics=sem, vmem_limit_bytes=VMEM_LIMIT)


def _dot(a, b):
    return jnp.dot(a, b, preferred_element_type=F32)


def _dot_nt(a, b):
    return lax.dot_general(a, b, (((1,), (1,)), ((), ())), preferred_element_type=F32)


def _group_of_block(i, rows_per_block, n_prompt, dec_seq):
    row = i * rows_per_block
    return jnp.where(row < n_prompt, 0, 1 + (row - n_prompt) // dec_seq)


def _norm_mod(x, g, shift, scale):
    y = x * lax.rsqrt(jnp.mean(x * x, axis=-1, keepdims=True) + EPS)
    return (y * g) * (1.0 + scale) + shift


def _log_sigmoid(x):
    return jnp.minimum(x, 0.0) - jnp.log(1.0 + jnp.exp(-jnp.abs(x)))


def _split3(x):
    hi = x.astype(BF16)
    r1 = x - hi.astype(F32)
    mid = r1.astype(BF16)
    lo = (r1 - mid.astype(F32)).astype(BF16)
    return hi, mid, lo


def _adaln_kernel(c_ref, w_ref, b_ref, o_ref):
    c = c_ref[...]
    s = (c * jax.nn.sigmoid(c)).astype(BF16)
    o_ref[...] = _dot(s, w_ref[...].astype(BF16)) + b_ref[...]


def _adaln(cvecs, w_mod, b_mod):
    depth, d, n = w_mod.shape
    tn = 1024
    return pl.pallas_call(
        _adaln_kernel,
        out_shape=jax.ShapeDtypeStruct((depth, N_GROUPS_PAD, n), F32),
        grid=(depth, n // tn),
        in_specs=[
            pl.BlockSpec((N_GROUPS_PAD, d), lambda l, j: (0, 0)),
            pl.BlockSpec((None, d, tn), lambda l, j: (l, 0, j)),
            pl.BlockSpec((None, 1, tn), lambda l, j: (l, 0, j)),
        ],
        out_specs=pl.BlockSpec((None, N_GROUPS_PAD, tn), lambda l, j: (l, 0, j)),
        compiler_params=_cparams(("parallel", "parallel")),
        name="adaln",
    )(cvecs, w_mod, b_mod.reshape(depth, 1, n))


def _mod_spec(layer, which, tm, n_prompt, dec_seq, d, ncols=None, with_j=False):
    if ncols is None:
        ncols = d
    if with_j:
        return pl.BlockSpec((None, None, None, 1, ncols),
                            lambda i, j: (layer, _group_of_block(i, tm, n_prompt, dec_seq), which, 0, 0))
    return pl.BlockSpec((None, None, None, 1, ncols),
                        lambda i: (layer, _group_of_block(i, tm, n_prompt, dec_seq), which, 0, 0))


def _mlstm_inproj_kernel(x_ref, g_ref, sh_ref, sc_ref, w_ref, wg_ref, bg_ref, o_ref, og_ref, h_scr,
                         *, nq_blocks, qscale):
    j = pl.program_id(1)

    @pl.when(j == 0)
    def _():
        h = _norm_mod(x_ref[...], g_ref[...], sh_ref[...], sc_ref[...]).astype(BF16)
        h_scr[...] = h
        og_ref[...] = _dot(h, wg_ref[...]) + bg_ref[...]

    acc = _dot(h_scr[...], w_ref[...])
    scale = jnp.where(j < nq_blocks, qscale, 1.0).astype(F32)
    o_ref[...] = (acc * scale).astype(o_ref.dtype)


def _mlstm_inproj(x, norm_g, mods, layer, w_main, w_gate, b_gate, dims):
    t, d = x.shape
    n = w_main.shape[1]
    tm, tn = 1024, 512
    dk = d // MLSTM_HEADS // 2
    kern = functools.partial(_mlstm_inproj_kernel, nq_blocks=(MLSTM_HEADS * dk) // tn, qscale=float(dk) ** -0.5)
    return pl.pallas_call(
        kern,
        out_shape=(jax.ShapeDtypeStruct((t, n), BF16), jax.ShapeDtypeStruct((t, LANES), F32)),
        grid=(t // tm, n // tn),
        in_specs=[
            pl.BlockSpec((tm, d), lambda i, j: (i, 0)),
            pl.BlockSpec((1, d), lambda i, j: (0, 0)),
            _mod_spec(layer, 0, tm, dims["P"], dims["dec_seq"], d, with_j=True),
            _mod_spec(layer, 1, tm, dims["P"], dims["dec_seq"], d, with_j=True),
            pl.BlockSpec((d, tn), lambda i, j: (0, j)),
            pl.BlockSpec((d, LANES), lambda i, j: (0, 0)),
            pl.BlockSpec((1, LANES), lambda i, j: (0, 0)),
        ],
        out_specs=(pl.BlockSpec((tm, tn), lambda i, j: (i, j)),
                   pl.BlockSpec((tm, LANES), lambda i, j: (i, 0))),
        scratch_shapes=[pltpu.VMEM((tm, d), BF16)],
        compiler_params=_cparams(("parallel", "arbitrary")),
        name="mlstm_inproj",
    )(x, norm_g, mods, mods, w_main, w_gate, b_gate)


def _mlstm_kernel(fblk, bblk, seq, first, last,
                  qf, kf, vf, gf, qb, kb, vb, gb, c0, n0, m0,
                  hf, hb, cfin, nfin, mfin, c_scr, n_scr, m_scr, *, L, NH, DK, DV):
    s = pl.program_id(0)

    @pl.when(first[s] == 1)
    def _():
        c_scr[...] = c0[...]
        n_scr[...] = n0[...]
        m_scr[...] = m0[...]

    row = lax.broadcasted_iota(jnp.int32, (L, L), 0)
    col = lax.broadcasted_iota(jnp.int32, (L, L), 1)
    tril = col <= row
    triu = col >= row
    tril_bf = tril.astype(BF16)
    triu_bf = triu.astype(BF16)

    dirs = ((qf, kf, vf, gf, hf, tril, tril_bf, triu_bf, L - 1),
            (qb, kb, vb, gb, hb, triu, triu_bf, tril_bf, 0))
    for d, (q_ref, k_ref, v_ref, g_ref, h_ref, keep, keep_bf, keep_t_bf, last_row) in enumerate(dirs):
        g = g_ref[...]
        lf = _log_sigmoid(g)
        l1, l2, l3 = _split3(lf)
        bcol = _dot(keep_bf, l1) + _dot(keep_bf, l2) + _dot(keep_bf, l3)
        g_t = g.T
        lf_t = _log_sigmoid(g_t)
        t1, t2, t3 = _split3(lf_t)
        brow = _dot(t1, keep_t_bf) + _dot(t2, keep_t_bf) + _dot(t3, keep_t_bf)
        base = 2 * NH * d
        for h in range(NH):
            ci = base + h
            cf = base + NH + h
            sr = NH * d + h
            li_c = g[:, ci:ci + 1]
            b_c = bcol[:, cf:cf + 1]
            li_r = g_t[ci:ci + 1, :]
            b_r = brow[cf:cf + 1, :]
            m_prev = m_scr[sr:sr + 1, 0:1]
            dmat = jnp.where(keep, (b_c - b_r) + li_r, -jnp.inf)
            gv = b_c + m_prev
            m_t = jnp.maximum(gv, jnp.max(dmat, axis=1, keepdims=True))
            w_inter = jnp.exp(gv - m_t)
            w_intra = jnp.exp(dmat - m_t)
            q = q_ref[:, h * DK:(h + 1) * DK]
            k = k_ref[:, h * DK:(h + 1) * DK]
            v = v_ref[:, h * DV:(h + 1) * DV]
            sc = _dot_nt(q, k) * w_intra
            c_st = c_scr[d, h]
            n_st = n_scr[sr:sr + 1, :]
            num = w_inter * _dot(q, c_st.astype(BF16)) + _dot(sc.astype(BF16), v)
            den = (w_inter * jnp.sum(q.astype(F32) * n_st, axis=1, keepdims=True)
                   + jnp.sum(sc, axis=1, keepdims=True))
            h_ref[:, h * DV:(h + 1) * DV] = num / jnp.maximum(jnp.abs(den), jnp.exp(-m_t))
            b_last = b_c[last_row:last_row + 1, :]
            a = (b_last - b_c) + li_c
            m_loc = jnp.max(a, axis=0, keepdims=True)
            w = jnp.exp(a - m_loc)
            kw = k.astype(F32) * w
            c_loc = _dot(kw.T.astype(BF16), v)
            n_loc = jnp.sum(kw, axis=0, keepdims=True)
            m_new = jnp.maximum(b_last + m_prev, m_loc)
            w_old = jnp.exp(b_last + m_prev - m_new)
            w_new = jnp.exp(m_loc - m_new)
            c_scr[d, h] = w_old * c_st + w_new * c_loc
            n_scr[sr:sr + 1, :] = w_old * n_st + w_new * n_loc
            m_scr[sr:sr + 1, :] = jnp.broadcast_to(m_new, (1, LANES))

    @pl.when(last[s] == 1)
    def _():
        cfin[...] = c_scr[...]
        nfin[...] = n_scr[...]
        mfin[...] = m_scr[...]


def _mlstm_scan(main, gates, c0, n0, m0, dims):
    t = main.shape[0]
    d_model = dims["D"]
    nh = MLSTM_HEADS
    dv = d_model // nh
    dk = dv // 2
    L = MLSTM_CHUNK
    assert dims["seq"] == L and dims["dec_seq"] % L == 0
    bp, bs = dims["Bp"], dims["Bs"]
    ncs = dims["dec_seq"] // L
    pblk = dims["P"] // L
    fblk = np.concatenate([np.arange(bp), pblk + np.arange(bs * ncs)]).astype(np.int32)
    bblk = np.concatenate([np.arange(bp),
                           pblk + (np.arange(bs)[:, None] * ncs + (ncs - 1 - np.arange(ncs))[None, :]).reshape(-1)]
                          ).astype(np.int32)
    seq = np.concatenate([np.arange(bp), bp + np.repeat(np.arange(bs), ncs)]).astype(np.int32)
    first = np.concatenate([np.ones(bp), (np.tile(np.arange(ncs), bs) == 0)]).astype(np.int32)
    last = np.concatenate([np.ones(bp), (np.tile(np.arange(ncs), bs) == ncs - 1)]).astype(np.int32)
    nseq = bp + bs
    nsteps = fblk.shape[0]
    qk = nh * dk
    vw = nh * dv
    kern = functools.partial(_mlstm_kernel, L=L, NH=nh, DK=dk, DV=dv)

    def fmap(cb):
        return lambda s, fb, bb, sq, fi, la: (fb[s], cb)

    def bmap(cb):
        return lambda s, fb, bb, sq, fi, la: (bb[s], cb)

    def smap(nd):
        return lambda s, fb, bb, sq, fi, la: (sq[s],) + (0,) * nd

    grid_spec = pltpu.PrefetchScalarGridSpec(
        num_scalar_prefetch=5,
        grid=(nsteps,),
        in_specs=[
            pl.BlockSpec((L, qk), fmap(0)), pl.BlockSpec((L, qk), fmap(1)),
            pl.BlockSpec((L, vw), fmap(2 * qk // vw)), pl.BlockSpec((L, LANES), fmap(0)),
            pl.BlockSpec((L, qk), bmap(0)), pl.BlockSpec((L, qk), bmap(1)),
            pl.BlockSpec((L, vw), bmap(2 * qk // vw)), pl.BlockSpec((L, LANES), bmap(0)),
            pl.BlockSpec((None, 2, nh, dk, dv), smap(4)),
            pl.BlockSpec((None, 2 * nh, dk), smap(2)),
            pl.BlockSpec((None, 2 * nh, LANES), smap(2)),
        ],
        out_specs=(
            pl.BlockSpec((L, vw), fmap(0)), pl.BlockSpec((L, vw), bmap(0)),
            pl.BlockSpec((None, 2, nh, dk, dv), smap(4)),
            pl.BlockSpec((None, 2 * nh, dk), smap(2)),
            pl.BlockSpec((None, 2 * nh, LANES), smap(2)),
        ),
        scratch_shapes=[pltpu.VMEM((2, nh, dk, dv), F32), pltpu.VMEM((2 * nh, dk), F32),
                        pltpu.VMEM((2 * nh, LANES), F32)],
    )
    return pl.pallas_call(
        kern,
        out_shape=(jax.ShapeDtypeStruct((t, vw), F32), jax.ShapeDtypeStruct((t, vw), F32),
                   jax.ShapeDtypeStruct((nseq, 2, nh, dk, dv), F32),
                   jax.ShapeDtypeStruct((nseq, 2 * nh, dk), F32),
                   jax.ShapeDtypeStruct((nseq, 2 * nh, LANES), F32)),
        grid_spec=grid_spec,
        compiler_params=_cparams(("arbitrary",)),
        name="mlstm_scan",
    )(jnp.asarray(fblk), jnp.asarray(bblk), jnp.asarray(seq), jnp.asarray(first), jnp.asarray(last),
      main, main, main, gates, main, main, main, gates, c0, n0, m0)


def _mlstm_out_kernel(x_ref, hf_ref, hb_ref, o_ref, ng_ref, w_ref, g1_ref, out_ref, *, NH, DV):
    parts = []
    for h in range(NH):
        sl = slice(h * DV, (h + 1) * DV)
        hs = hf_ref[:, sl] + hb_ref[:, sl]
        hs = hs * lax.rsqrt(jnp.mean(hs * hs, axis=-1, keepdims=True) + EPS)
        hs = hs * ng_ref[:, sl]
        parts.append((hs * jax.nn.sigmoid(o_ref[:, sl].astype(F32))).astype(BF16))
    z = jnp.concatenate(parts, axis=1)
    out_ref[...] = x_ref[...] + g1_ref[...] * _dot(z, w_ref[...])


def _mlstm_out(x, hf, hb, main, norm_g, w_out, mods, layer, dims):
    t, d = x.shape
    tm = 256
    vw = hf.shape[1]
    kern = functools.partial(_mlstm_out_kernel, NH=MLSTM_HEADS, DV=vw // MLSTM_HEADS)
    ocol = (main.shape[1] - vw) // vw
    return pl.pallas_call(
        kern,
        out_shape=jax.ShapeDtypeStruct((t, d), F32),
        grid=(t // tm,),
        in_specs=[
            pl.BlockSpec((tm, d), lambda i: (i, 0)),
            pl.BlockSpec((tm, vw), lambda i: (i, 0)),
            pl.BlockSpec((tm, vw), lambda i: (i, 0)),
            pl.BlockSpec((tm, vw), lambda i: (i, ocol)),
            pl.BlockSpec((1, vw), lambda i: (0, 0)),
            pl.BlockSpec((vw, d), lambda i: (0, 0)),
            _mod_spec(layer, 2, tm, dims["P"], dims["dec_seq"], d),
        ],
        out_specs=pl.BlockSpec((tm, d), lambda i: (i, 0)),
        compiler_params=_cparams(("parallel",)),
        name="mlstm_out",
    )(x, hf, hb, main, norm_g, w_out, mods)


def _attn_inproj_kernel(x_ref, g_ref, sh_ref, sc_ref, w_ref, qn_ref, kn_ref, cos_ref, sin_ref,
                        oq_ref, okv_ref, h_scr, *, nq_blocks, heads_per_block, DH):
    j = pl.program_id(1)

    @pl.when(j == 0)
    def _():
        h_scr[...] = _norm_mod(x_ref[...], g_ref[...], sh_ref[...], sc_ref[...]).astype(BF16)

    acc = _dot(h_scr[...], w_ref[...])

    def normed(gain):
        cos = cos_ref[...]
        sin = sin_ref[...]
        lane = lax.broadcasted_iota(jnp.int32, cos.shape, 1)
        first_half = (lane % (DH // 2)) < (DH // 4)
        outs = []
        for hh in range(heads_per_block):
            a = acc[:, hh * DH:(hh + 1) * DH]
            r = a * lax.rsqrt(jnp.mean(a * a, axis=-1, keepdims=True) + EPS) * gain
            partner = jnp.where(first_half, pltpu.roll(r, DH - DH // 4, 1), pltpu.roll(r, DH // 4, 1))
            outs.append(r * cos + partner * sin)
        return jnp.concatenate(outs, axis=1)

    @pl.when(j < nq_blocks)
    def _():
        oq_ref[...] = normed(qn_ref[...]).astype(oq_ref.dtype)

    @pl.when(j == nq_blocks)
    def _():
        okv_ref[...] = normed(kn_ref[...])

    @pl.when(j > nq_blocks)
    def _():
        okv_ref[...] = acc


def _attn_inproj(x, norm_g, mods, layer, w_in, qn_g, kn_g, cos_t, sin_t, dims):
    t, d = x.shape
    n = w_in.shape[1]
    dh = d // ATTN_HEADS
    tm, tn = 1024, 512
    assert ATTN_KV_HEADS * dh == tn
    nqb = (ATTN_HEADS * dh) // tn
    kern = functools.partial(_attn_inproj_kernel, nq_blocks=nqb, heads_per_block=tn // dh, DH=dh)
    return pl.pallas_call(
        kern,
        out_shape=(jax.ShapeDtypeStruct((t, ATTN_HEADS * dh), BF16),
                   jax.ShapeDtypeStruct((t, 2 * ATTN_KV_HEADS * dh), F32)),
        grid=(t // tm, n // tn),
        in_specs=[
            pl.BlockSpec((tm, d), lambda i, j: (i, 0)),
            pl.BlockSpec((1, d), lambda i, j: (0, 0)),
            _mod_spec(layer, 0, tm, dims["P"], dims["dec_seq"], d, with_j=True),
            _mod_spec(layer, 1, tm, dims["P"], dims["dec_seq"], d, with_j=True),
            pl.BlockSpec((d, tn), lambda i, j: (0, j)),
            pl.BlockSpec((1, dh), lambda i, j: (0, 0)),
            pl.BlockSpec((1, dh), lambda i, j: (0, 0)),
            pl.BlockSpec((tm, dh), lambda i, j: (i, 0)),
            pl.BlockSpec((tm, dh), lambda i, j: (i, 0)),
        ],
        out_specs=(pl.BlockSpec((tm, tn), lambda i, j: (i, jnp.minimum(j, nqb - 1))),
                   pl.BlockSpec((tm, tn), lambda i, j: (i, jnp.maximum(j - nqb, 0)))),
        scratch_shapes=[pltpu.VMEM((tm, d), BF16)],
        compiler_params=_cparams(("parallel", "arbitrary")),
        name="attn_inproj",
    )(x, norm_g, mods, mods, w_in, qn_g, kn_g, cos_t, sin_t)


def _softmax_sink_rows(s_list, sink_col):
    m = sink_col
    for s in s_list:
        m = jnp.maximum(m, jnp.max(s, axis=1, keepdims=True))
    ps = [jnp.exp(s - m) for s in s_list]
    tot = jnp.exp(sink_col - m)
    for p in ps:
        tot = tot + jnp.sum(p, axis=1, keepdims=True)
    return [p / tot for p in ps]


def _ctx_attn_kernel(sink_ref, q_ref, k_ref, v_ref, o_ref, *, G, DH):
    kv = pl.program_id(1)
    k = k_ref[...].astype(BF16)
    v = v_ref[...].astype(BF16)
    scale = float(DH) ** -0.5
    for g in range(G):
        q = q_ref[:, g * DH:(g + 1) * DH]
        s = _dot_nt(q, k) * scale
        sink = jnp.full((s.shape[0], 1), sink_ref[kv * G + g], F32)
        (p,) = _softmax_sink_rows([s], sink)
        o_ref[:, g * DH:(g + 1) * DH] = _dot(p.astype(BF16), v).astype(o_ref.dtype)


def _lat_attn_kernel(sink_ref, q_ref, kp_ref, kc_ref, kn_ref, vp_ref, vc_ref, vn_ref, kx_ref, vx_ref, o_ref,
                     *, G, DH, NB, BLK):
    kv = pl.program_id(1)
    i = pl.program_id(2)
    scale = float(DH) ** -0.5
    q = jnp.concatenate([q_ref[:, g * DH:(g + 1) * DH] for g in range(G)], axis=0)
    kw = jnp.concatenate([kp_ref[...], kc_ref[...], kn_ref[...]], axis=0).astype(BF16)
    vw = jnp.concatenate([vp_ref[...], vc_ref[...], vn_ref[...]], axis=0).astype(BF16)
    kx = kx_ref[...].astype(BF16)
    vx = vx_ref[...].astype(BF16)
    rows = G * BLK
    r = lax.broadcasted_iota(jnp.int32, (rows, 3 * BLK), 0) % BLK
    c = lax.broadcasted_iota(jnp.int32, (rows, 3 * BLK), 1)
    valid = (jnp.abs(c - BLK - r) <= WINDOW) & ((c >= BLK) | (i > 0)) & ((c < 2 * BLK) | (i < NB - 1))
    s_w = jnp.where(valid, _dot_nt(q, kw) * scale, -jnp.inf)
    s_c = _dot_nt(q, kx) * scale
    rid = lax.broadcasted_iota(jnp.int32, (rows, 1), 0) // BLK
    sink = jnp.zeros((rows, 1), F32)
    for g in range(G):
        sink = jnp.where(rid == g, sink_ref[kv * G + g], sink)
    p_w, p_c = _softmax_sink_rows([s_w, s_c], sink)
    o = _dot(p_w.astype(BF16), vw) + _dot(p_c.astype(BF16), vx)
    for g in range(G):
        o_ref[:, g * DH:(g + 1) * DH] = o[g * BLK:(g + 1) * BLK, :].astype(o_ref.dtype)


def _attention(qh, kvh, cache_k, cache_v, sink, dims):
    t = qh.shape[0]
    dh = qh.shape[1] // ATTN_HEADS
    G, KV = ATTN_GROUP, ATTN_KV_HEADS
    bp, bs, seq, dec_seq = dims["Bp"], dims["Bs"], dims["seq"], dims["dec_seq"]
    gw = G * dh
    o_ctx = pl.pallas_call(
        functools.partial(_ctx_attn_kernel, G=G, DH=dh),
        out_shape=jax.ShapeDtypeStruct((dims["P"], ATTN_HEADS * dh), BF16),
        grid_spec=pltpu.PrefetchScalarGridSpec(
            num_scalar_prefetch=1,
            grid=(bp, KV),
            in_specs=[
                pl.BlockSpec((seq, gw), lambda b, kv, sk: (b, kv)),
                pl.BlockSpec((seq, dh), lambda b, kv, sk: (b, kv)),
                pl.BlockSpec((seq, dh), lambda b, kv, sk: (b, KV + kv)),
            ],
            out_specs=pl.BlockSpec((seq, gw), lambda b, kv, sk: (b, kv)),
        ),
        compiler_params=_cparams(("parallel", "parallel")),
        name="ctx_attn",
    )(sink, qh, kvh, kvh)
    blk = ATTN_BLOCK
    assert blk == WINDOW and dec_seq % blk == 0 and dims["P"] % blk == 0
    nb = dec_seq // blk
    off = dims["P"] // blk

    def qmap(r, kv, i, sk):
        return (off + r * nb + i, kv)

    def omap(r, kv, i, sk):
        return (r * nb + i, kv)

    def kmap(delta, colbase):
        def f(r, kv, i, sk):
            return (off + r * nb + jnp.clip(i + delta, 0, nb - 1), colbase + kv)
        return f

    o_lat = pl.pallas_call(
        functools.partial(_lat_attn_kernel, G=G, DH=dh, NB=nb, BLK=blk),
        out_shape=jax.ShapeDtypeStruct((dims["S"], ATTN_HEADS * dh), BF16),
        grid_spec=pltpu.PrefetchScalarGridSpec(
            num_scalar_prefetch=1,
            grid=(bs, KV, nb),
            in_specs=[
                pl.BlockSpec((blk, gw), qmap),
                pl.BlockSpec((blk, dh), kmap(-1, 0)), pl.BlockSpec((blk, dh), kmap(0, 0)),
                pl.BlockSpec((blk, dh), kmap(1, 0)),
                pl.BlockSpec((blk, dh), kmap(-1, KV)), pl.BlockSpec((blk, dh), kmap(0, KV)),
                pl.BlockSpec((blk, dh), kmap(1, KV)),
                pl.BlockSpec((None, None, cache_k.shape[2], dh), lambda r, kv, i, sk: (r, kv, 0, 0)),
                pl.BlockSpec((None, None, cache_v.shape[2], dh), lambda r, kv, i, sk: (r, kv, 0, 0)),
            ],
            out_specs=pl.BlockSpec((blk, gw), omap),
        ),
        compiler_params=_cparams(("parallel", "parallel", "arbitrary")),
        name="lat_attn",
    )(sink, qh, kvh, kvh, kvh, kvh, kvh, kvh, cache_k, cache_v)
    return jnp.concatenate([o_ctx, o_lat], axis=0)


def _proj_res_kernel(x_ref, a_ref, w_ref, g1_ref, out_ref):
    out_ref[...] = x_ref[...] + g1_ref[...] * _dot(a_ref[...], w_ref[...])


def _proj_res(x, a, w, mods, layer, which, dims):
    t, d = x.shape
    tm = 512
    return pl.pallas_call(
        _proj_res_kernel,
        out_shape=jax.ShapeDtypeStruct((t, d), F32),
        grid=(t // tm,),
        in_specs=[
            pl.BlockSpec((tm, d), lambda i: (i, 0)),
            pl.BlockSpec((tm, a.shape[1]), lambda i: (i, 0)),
            pl.BlockSpec(w.shape, lambda i: (0, 0)),
            _mod_spec(layer, which, tm, dims["P"], dims["dec_seq"], d),
        ],
        out_specs=pl.BlockSpec((tm, d), lambda i: (i, 0)),
        compiler_params=_cparams(("parallel",)),
        name="proj_res",
    )(x, a, w, mods)


def _h2_router_kernel(x_ref, g_ref, sh_ref, sc_ref, wr_ref, h_ref, lg_ref):
    h = _norm_mod(x_ref[...], g_ref[...], sh_ref[...], sc_ref[...])
    h_ref[...] = h
    lg_ref[...] = _dot(h.astype(BF16), wr_ref[...])


def _h2_router(x, norm_g, mods, layer, w_router_pad, dims):
    t, d = x.shape
    tm = 512
    return pl.pallas_call(
        _h2_router_kernel,
        out_shape=(jax.ShapeDtypeStruct((t, d), F32), jax.ShapeDtypeStruct((t, LANES), F32)),
        grid=(t // tm,),
        in_specs=[
            pl.BlockSpec((tm, d), lambda i: (i, 0)),
            pl.BlockSpec((1, d), lambda i: (0, 0)),
            _mod_spec(layer, 3, tm, dims["P"], dims["dec_seq"], d),
            _mod_spec(layer, 4, tm, dims["P"], dims["dec_seq"], d),
            pl.BlockSpec((d, LANES), lambda i: (0, 0)),
        ],
        out_specs=(pl.BlockSpec((tm, d), lambda i: (i, 0)), pl.BlockSpec((tm, LANES), lambda i: (i, 0))),
        compiler_params=_cparams(("parallel",)),
        name="h2_router",
    )(x, norm_g, mods, mods, w_router_pad)


def _gather_kernel(idx_ref, h_hbm, o_ref, sem, *, TG):
    base = pl.program_id(0) * TG

    def issue(r, carry):
        tok = idx_ref[base + r]
        pltpu.make_async_copy(h_hbm.at[pl.ds(tok, 1), :], o_ref.at[pl.ds(r, 1), :], sem).start()
        return carry

    lax.fori_loop(0, TG, issue, 0)
    pltpu.make_async_copy(h_hbm.at[pl.ds(0, TG), :], o_ref, sem).wait()


def _moe_gather(h2, idx_flat):
    t, d = h2.shape
    n = idx_flat.shape[0]
    tg = 256
    return pl.pallas_call(
        functools.partial(_gather_kernel, TG=tg),
        out_shape=jax.ShapeDtypeStruct((n, d), F32),
        grid_spec=pltpu.PrefetchScalarGridSpec(
            num_scalar_prefetch=1,
            grid=(n // tg,),
            in_specs=[pl.BlockSpec(memory_space=pl.ANY)],
            out_specs=pl.BlockSpec((tg, d), lambda i, idx: (i, 0)),
            scratch_shapes=[pltpu.SemaphoreType.DMA],
        ),
        compiler_params=_cparams(("arbitrary",)),
        name="moe_gather",
    )(idx_flat, h2)


def _ffn_kernel(x_ref, wg_ref, wu_ref, wd_ref, gate_ref, o_ref, *, NF):
    f = pl.program_id(2)
    x = x_ref[...].astype(BF16)
    a = _dot(x, wg_ref[...].astype(BF16))
    u = _dot(x, wu_ref[...].astype(BF16))
    hmid = (a * jax.nn.sigmoid(a) * u).astype(BF16)
    y = _dot(hmid, wd_ref[...].astype(BF16))

    @pl.when(f == 0)
    def _():
        o_ref[...] = y

    @pl.when(f > 0)
    def _():
        o_ref[...] += y

    @pl.when(f == NF - 1)
    def _():
        o_ref[...] = o_ref[...] * gate_ref[...]


def _moe_ffn(xg, w_gate, w_up, w_down, gates_col, layer):
    n, d = xg.shape
    e = N_EXPERTS
    dff = w_gate.shape[-1]
    rows = n // e
    tm = next(c for c in (1024, 512, 256) if rows % c == 0)
    tf = 256
    mt = rows // tm
    nf = dff // tf
    return pl.pallas_call(
        functools.partial(_ffn_kernel, NF=nf),
        out_shape=jax.ShapeDtypeStruct((n, d), F32),
        grid=(e, mt, nf),
        in_specs=[
            pl.BlockSpec((tm, d), lambda ex, m, f: (ex * mt + m, 0)),
            pl.BlockSpec((None, None, d, tf), lambda ex, m, f: (layer, ex, 0, f)),
            pl.BlockSpec((None, None, d, tf), lambda ex, m, f: (layer, ex, 0, f)),
            pl.BlockSpec((None, None, tf, d), lambda ex, m, f: (layer, ex, f, 0)),
            pl.BlockSpec((tm, 1), lambda ex, m, f: (ex * mt + m, 0)),
        ],
        out_specs=pl.BlockSpec((tm, d), lambda ex, m, f: (ex * mt + m, 0)),
        compiler_params=_cparams(("parallel", "parallel", "arbitrary")),
        name="moe_ffn",
    )(xg, w_gate, w_up, w_down, gates_col)


def _combine_kernel(idx_ref, y_ref, g2_ref, x_hbm, o_hbm, buf, sem_in, sem_out, *, TG):
    base = pl.program_id(0) * TG

    def issue_in(r, carry):
        tok = idx_ref[base + r]
        pltpu.make_async_copy(o_hbm.at[pl.ds(tok, 1), :], buf.at[pl.ds(r, 1), :], sem_in).start()
        return carry

    lax.fori_loop(0, TG, issue_in, 0)
    pltpu.make_async_copy(o_hbm.at[pl.ds(0, TG), :], buf, sem_in).wait()
    buf[...] = buf[...] + g2_ref[...] * y_ref[...]

    def issue_out(r, carry):
        tok = idx_ref[base + r]
        pltpu.make_async_copy(buf.at[pl.ds(r, 1), :], o_hbm.at[pl.ds(tok, 1), :], sem_out).start()
        return carry

    lax.fori_loop(0, TG, issue_out, 0)
    pltpu.make_async_copy(buf, o_hbm.at[pl.ds(0, TG), :], sem_out).wait()


def _moe_combine(x, yg, idx_flat, mods, layer, grp_of_tile):
    t, d = x.shape
    n = yg.shape[0]
    tg = 256
    return pl.pallas_call(
        functools.partial(_combine_kernel, TG=tg),
        out_shape=jax.ShapeDtypeStruct((t, d), F32),
        grid_spec=pltpu.PrefetchScalarGridSpec(
            num_scalar_prefetch=1,
            grid=(n // tg,),
            in_specs=[
                pl.BlockSpec((tg, d), lambda i, idx: (i, 0)),
                pl.BlockSpec((None, None, None, 1, d), lambda i, idx: (layer, grp_of_tile(i), 5, 0, 0)),
                pl.BlockSpec(memory_space=pl.ANY),
            ],
            out_specs=pl.BlockSpec(memory_space=pl.ANY),
            scratch_shapes=[pltpu.VMEM((tg, d), F32), pltpu.SemaphoreType.DMA, pltpu.SemaphoreType.DMA],
        ),
        input_output_aliases={3: 0},
        compiler_params=_cparams(("arbitrary",)),
        name="moe_combine",
    )(idx_flat, yg, mods, x)


def _route(logits, dims):
    e = N_EXPERTS
    bp, bs, seq, dec_seq, p = dims["Bp"], dims["Bs"], dims["seq"], dims["dec_seq"], dims["P"]
    aff = jax.nn.softmax(logits[:, :e], axis=-1)
    cap_p = CAPACITY_FACTOR * seq // e
    cap_s = CAPACITY_FACTOR * dec_seq // e
    gp, ip = lax.top_k(jnp.swapaxes(aff[:p].reshape(bp, seq, e), 1, 2), cap_p)
    gs, is_ = lax.top_k(jnp.swapaxes(aff[p:].reshape(bs, dec_seq, e), 1, 2), cap_s)
    ip = ip + (jnp.arange(bp, dtype=jnp.int32) * seq)[:, None, None]
    is_ = is_ + (p + jnp.arange(bs, dtype=jnp.int32) * dec_seq)[:, None, None]
    idx = jnp.concatenate([jnp.swapaxes(ip, 0, 1).reshape(e, bp * cap_p),
                           jnp.swapaxes(is_, 0, 1).reshape(e, bs * cap_s)], axis=1)
    gate = jnp.concatenate([jnp.swapaxes(gp, 0, 1).reshape(e, bp * cap_p),
                            jnp.swapaxes(gs, 0, 1).reshape(e, bs * cap_s)], axis=1)
    return idx.astype(jnp.int32), gate


def _moe(x, norm_g, mods, layer, w_router_pad, w_gate, w_up, w_down, dims):
    e = N_EXPERTS
    h2, logits = _h2_router(x, norm_g, mods, layer, w_router_pad, dims)
    idx, gate = _route(logits, dims)
    rows = idx.shape[1]
    idx_flat = idx.reshape(-1)
    xg = _moe_gather(h2, idx_flat)
    yg = _moe_ffn(xg, w_gate, w_up, w_down, gate.reshape(-1, 1), layer)
    tg = 256
    tiles_per_e = rows // tg
    rows_p = dims["Bp"] * (CAPACITY_FACTOR * dims["seq"] // e)
    cap_s = CAPACITY_FACTOR * dims["dec_seq"] // e
    assert rows_p % tg == 0 and cap_s % tg == 0

    def grp_of_tile(i):
        r = (i % tiles_per_e) * tg
        return jnp.where(r < rows_p, 0, 1 + (r - rows_p) // cap_s)

    return _moe_combine(x, yg, idx_flat, mods, layer, grp_of_tile)


def _rope_tables(dims, dh):
    n, p = dims["dec_seq"], dims["P"]
    axis_dim = dh // 2
    tpos = jnp.arange(n)
    rowp = (tpos // GRID_W).astype(F32)
    colp = (tpos % GRID_W).astype(F32)
    inv = ROPE_BASE ** (-jnp.arange(0, axis_dim, 2, dtype=F32) / axis_dim)
    ar = rowp[:, None] * inv
    ac = colp[:, None] * inv
    cos = jnp.concatenate([jnp.cos(ar), jnp.cos(ar), jnp.cos(ac), jnp.cos(ac)], axis=1)
    sin = jnp.concatenate([-jnp.sin(ar), jnp.sin(ar), -jnp.sin(ac), jnp.sin(ac)], axis=1)
    cos = jnp.concatenate([jnp.ones((p, dh), F32), jnp.tile(cos, (dims["Bs"], 1))], axis=0)
    sin = jnp.concatenate([jnp.zeros((p, dh), F32), jnp.tile(sin, (dims["Bs"], 1))], axis=0)
    return cos, sin


def kernel(x_prompt, x_sample, state_mlstm_C, state_mlstm_n, state_mlstm_m, cache_attn_k, cache_attn_v,
           c, c_ctx, w_mod, b_mod, norm1_g, norm2_g, m_w_in, m_b_gate, m_norm_g, m_w_out,
           a_w_in, a_q_norm_g, a_k_norm_g, a_sink, a_w_out, e_w_router, e_w_gate, e_w_up, e_w_down):
    bp, seq, d = x_prompt.shape
    bs, dec_seq, _ = x_sample.shape
    depth = w_mod.shape[0]
    dims = dict(Bp=bp, Bs=bs, seq=seq, dec_seq=dec_seq, P=bp * seq, S=bs * dec_seq, D=d)
    nh = MLSTM_HEADS
    dv = d // nh
    dk = dv // 2
    qk, vw = nh * dk, nh * dv
    dh = d // ATTN_HEADS

    x = jnp.concatenate([x_prompt.reshape(bp * seq, d), x_sample.reshape(bs * dec_seq, d)], axis=0)

    cvecs = jnp.concatenate([c_ctx[None, :], c, jnp.zeros((N_GROUPS_PAD - 1 - bs, d), F32)], axis=0)
    mods = _adaln(cvecs, w_mod, b_mod).reshape(depth, N_GROUPS_PAD, N_MOD, 1, d)
    cos_t, sin_t = _rope_tables(dims, dh)

    new_c, new_n, new_m, new_k, new_v = [], [], [], [], []
    for l in range(depth):
        j = l // 2
        if l % 2 == 0:
            w_in = m_w_in[j]
            w_main = w_in[:, :2 * qk + 2 * vw].astype(BF16)
            w_gate = jnp.pad(w_in[:, 2 * qk + 2 * vw:], ((0, 0), (0, LANES - 4 * nh))).astype(BF16)
            b_gate = jnp.pad(m_b_gate[j], (0, LANES - 4 * nh)).reshape(1, LANES)
            main, gates = _mlstm_inproj(x, norm1_g[l].reshape(1, d), mods, l, w_main, w_gate, b_gate, dims)
            c0 = jnp.concatenate([jnp.zeros((bp, 2, nh, dk, dv), F32), state_mlstm_C[:, j]], axis=0)
            n0 = jnp.concatenate([jnp.zeros((bp, 2 * nh, dk), F32),
                                  state_mlstm_n[:, j].reshape(bs, 2 * nh, dk)], axis=0)
            m0 = jnp.concatenate([jnp.zeros((bp, 2 * nh), F32), state_mlstm_m[:, j].reshape(bs, 2 * nh)], axis=0)
            m0 = jnp.broadcast_to(m0[:, :, None], (bp + bs, 2 * nh, LANES))
            hf, hb, cfin, nfin, mfin = _mlstm_scan(main, gates, c0, n0, m0, dims)
            new_c.append(cfin[:bp])
            new_n.append(nfin[:bp].reshape(bp, 2, nh, dk))
            new_m.append(mfin[:bp, :, 0].reshape(bp, 2, nh))
            x = _mlstm_out(x, hf, hb, main, m_norm_g[j].reshape(1, vw), m_w_out[j].astype(BF16), mods, l, dims)
        else:
            qh, kvh = _attn_inproj(x, norm1_g[l].reshape(1, d), mods, l, a_w_in[j].astype(BF16),
                                   a_q_norm_g[j].reshape(1, dh), a_k_norm_g[j].reshape(1, dh), cos_t, sin_t, dims)
            kvw = ATTN_KV_HEADS * dh
            new_k.append(kvh[:bp * seq, :kvw].reshape(bp, seq, ATTN_KV_HEADS, dh).transpose(0, 2, 1, 3))
            new_v.append(kvh[:bp * seq, kvw:].reshape(bp, seq, ATTN_KV_HEADS, dh).transpose(0, 2, 1, 3))
            o = _attention(qh, kvh, cache_attn_k[:, j], cache_attn_v[:, j], a_sink[j], dims)
            x = _proj_res(x, o, a_w_out[j].astype(BF16), mods, l, 2, dims)
        w_router_pad = jnp.pad(e_w_router[l], ((0, 0), (0, LANES - N_EXPERTS))).astype(BF16)
        x = _moe(x, norm2_g[l].reshape(1, d), mods, l, w_router_pad, e_w_gate, e_w_up, e_w_down, dims)

    p = bp * seq
    return (x[:p].reshape(bp, seq, d), x[p:].reshape(bs, dec_seq, d),
            jnp.stack(new_c, axis=1), jnp.stack(new_n, axis=1), jnp.stack(new_m, axis=1),
            jnp.stack(new_k, axis=1), jnp.stack(new_v, axis=1))
```

```python
import functools

import jax
import jax.numpy as jnp
import numpy as np
from jax import lax
from jax.experimental import pallas as pl
from jax.experimental.pallas import tpu as pltpu

F32 = jnp.float32
BF16 = jnp.bfloat16

GRID_W = 64
MLSTM_HEADS = 8
ATTN_HEADS = 16
ATTN_KV_HEADS = 4
ATTN_GROUP = ATTN_HEADS // ATTN_KV_HEADS
WINDOW = 128
ROPE_BASE = 10000.0
N_EXPERTS = 16
CAPACITY_FACTOR = 2
N_MOD = 6
EPS = 1e-6
N_GROUPS_PAD = 8
LANES = 128
MLSTM_CHUNK = 256
ATTN_BLOCK = 128
VMEM_LIMIT = 56 * 1024 * 1024


def _cparams(sem):
    return pltpu.CompilerParams(dimension_semantics=sem, vmem_limit_bytes=VMEM_LIMIT)


def _dot(a, b):
    return jnp.dot(a, b, preferred_element_type=F32)


def _dot_nt(a, b):
    return lax.dot_general(a, b, (((1,), (1,)), ((), ())), preferred_element_type=F32)


def _group_of_block(i, rows_per_block, n_prompt, dec_seq):
    row = i * rows_per_block
    return jnp.where(row < n_prompt, 0, 1 + (row - n_prompt) // dec_seq)


def _norm_mod(x, g, shift, scale):
    y = x * lax.rsqrt(jnp.mean(x * x, axis=-1, keepdims=True) + EPS)
    return (y * g) * (1.0 + scale) + shift


def _log_sigmoid(x):
    return jnp.minimum(x, 0.0) - jnp.log(1.0 + jnp.exp(-jnp.abs(x)))


def _split3(x):
    hi = x.astype(BF16)
    r1 = x - hi.astype(F32)
    mid = r1.astype(BF16)
    lo = (r1 - mid.astype(F32)).astype(BF16)
    return hi, mid, lo


def _adaln_kernel(c_ref, w_ref, b_ref, o_ref):
    c = c_ref[...]
    s = (c * jax.nn.sigmoid(c)).astype(BF16)
    o_ref[...] = _dot(s, w_ref[...].astype(BF16)) + b_ref[...]


def _adaln(cvecs, w_mod, b_mod):
    depth, d, n = w_mod.shape
    tn = 1024
    return pl.pallas_call(
        _adaln_kernel,
        out_shape=jax.ShapeDtypeStruct((depth, N_GROUPS_PAD, n), F32),
        grid=(depth, n // tn),
        in_specs=[
            pl.BlockSpec((N_GROUPS_PAD, d), lambda l, j: (0, 0)),
            pl.BlockSpec((None, d, tn), lambda l, j: (l, 0, j)),
            pl.BlockSpec((None, 1, tn), lambda l, j: (l, 0, j)),
        ],
        out_specs=pl.BlockSpec((None, N_GROUPS_PAD, tn), lambda l, j: (l, 0, j)),
        compiler_params=_cparams(("parallel", "parallel")),
        name="adaln",
    )(cvecs, w_mod, b_mod.reshape(depth, 1, n))


def _mod_spec(layer, which, tm, n_prompt, dec_seq, d, ncols=None, with_j=False):
    if ncols is None:
        ncols = d
    if with_j:
        return pl.BlockSpec((None, None, None, 1, ncols),
                            lambda i, j: (layer, _group_of_block(i, tm, n_prompt, dec_seq), which, 0, 0))
    return pl.BlockSpec((None, None, None, 1, ncols),
                        lambda i: (layer, _group_of_block(i, tm, n_prompt, dec_seq), which, 0, 0))


def _mlstm_inproj_kernel(x_ref, g_ref, sh_ref, sc_ref, w_ref, wg_ref, bg_ref, o_ref, og_ref, h_scr,
                         *, nq_blocks, qscale):
    j = pl.program_id(1)

    @pl.when(j == 0)
    def _():
        h = _norm_mod(x_ref[...], g_ref[...], sh_ref[...], sc_ref[...]).astype(BF16)
        h_scr[...] = h
        og_ref[...] = _dot(h, wg_ref[...]) + bg_ref[...]

    acc = _dot(h_scr[...], w_ref[...])
    scale = jnp.where(j < nq_blocks, qscale, 1.0).astype(F32)
    o_ref[...] = (acc * scale).astype(o_ref.dtype)


def _mlstm_inproj(x, norm_g, mods, layer, w_main, w_gate, b_gate, dims):
    t, d = x.shape
    n = w_main.shape[1]
    tm, tn = 1024, 512
    dk = d // MLSTM_HEADS // 2
    kern = functools.partial(_mlstm_inproj_kernel, nq_blocks=(MLSTM_HEADS * dk) // tn, qscale=float(dk) ** -0.5)
    return pl.pallas_call(
        kern,
        out_shape=(jax.ShapeDtypeStruct((t, n), BF16), jax.ShapeDtypeStruct((t, LANES), F32)),
        grid=(t // tm, n // tn),
        in_specs=[
            pl.BlockSpec((tm, d), lambda i, j: (i, 0)),
            pl.BlockSpec((1, d), lambda i, j: (0, 0)),
            _mod_spec(layer, 0, tm, dims["P"], dims["dec_seq"], d, with_j=True),
            _mod_spec(layer, 1, tm, dims["P"], dims["dec_seq"], d, with_j=True),
            pl.BlockSpec((d, tn), lambda i, j: (0, j)),
            pl.BlockSpec((d, LANES), lambda i, j: (0, 0)),
            pl.BlockSpec((1, LANES), lambda i, j: (0, 0)),
        ],
        out_specs=(pl.BlockSpec((tm, tn), lambda i, j: (i, j)),
                   pl.BlockSpec((tm, LANES), lambda i, j: (i, 0))),
        scratch_shapes=[pltpu.VMEM((tm, d), BF16)],
        compiler_params=_cparams(("parallel", "arbitrary")),
        name="mlstm_inproj",
    )(x, norm_g, mods, mods, w_main, w_gate, b_gate)


def _mlstm_kernel(fblk, bblk, seq, first, last,
                  qf, kf, vf, gf, qb, kb, vb, gb, c0, n0, m0,
                  hf, hb, cfin, nfin, mfin, c_scr, n_scr, m_scr, *, L, NH, DK, DV):
    s = pl.program_id(0)

    @pl.when(first[s] == 1)
    def _():
        c_scr[...] = c0[...]
        n_scr[...] = n0[...]
        m_scr[...] = m0[...]

    row = lax.broadcasted_iota(jnp.int32, (L, L), 0)
    col = lax.broadcasted_iota(jnp.int32, (L, L), 1)
    tril = col <= row
    triu = col >= row
    tril_bf = tril.astype(BF16)
    triu_bf = triu.astype(BF16)

    dirs = ((qf, kf, vf, gf, hf, tril, tril_bf, triu_bf, L - 1),
            (qb, kb, vb, gb, hb, triu, triu_bf, tril_bf, 0))
    for d, (q_ref, k_ref, v_ref, g_ref, h_ref, keep, keep_bf, keep_t_bf, last_row) in enumerate(dirs):
        g = g_ref[...]
        lf = _log_sigmoid(g)
        l1, l2, l3 = _split3(lf)
        bcol = _dot(keep_bf, l1) + _dot(keep_bf, l2) + _dot(keep_bf, l3)
        g_t = g.T
        lf_t = _log_sigmoid(g_t)
        t1, t2, t3 = _split3(lf_t)
        brow = _dot(t1, keep_t_bf) + _dot(t2, keep_t_bf) + _dot(t3, keep_t_bf)
        base = 2 * NH * d
        for h in range(NH):
            ci = base + h
            cf = base + NH + h
            sr = NH * d + h
            li_c = g[:, ci:ci + 1]
            b_c = bcol[:, cf:cf + 1]
            li_r = g_t[ci:ci + 1, :]
            b_r = brow[cf:cf + 1, :]
            m_prev = m_scr[sr:sr + 1, 0:1]
            dmat = jnp.where(keep, (b_c - b_r) + li_r, -jnp.inf)
            gv = b_c + m_prev
            m_t = jnp.maximum(gv, jnp.max(dmat, axis=1, keepdims=True))
            w_inter = jnp.exp(gv - m_t)
            w_intra = jnp.exp(dmat - m_t)
            q = q_ref[:, h * DK:(h + 1) * DK]
            k = k_ref[:, h * DK:(h + 1) * DK]
            v = v_ref[:, h * DV:(h + 1) * DV]
            sc = _dot_nt(q, k) * w_intra
            c_st = c_scr[d, h]
            n_st = n_scr[sr:sr + 1, :]
            num = w_inter * _dot(q, c_st.astype(BF16)) + _dot(sc.astype(BF16), v)
            den = (w_inter * jnp.sum(q.astype(F32) * n_st, axis=1, keepdims=True)
                   + jnp.sum(sc, axis=1, keepdims=True))
            h_ref[:, h * DV:(h + 1) * DV] = num / jnp.maximum(jnp.abs(den), jnp.exp(-m_t))
            b_last = b_c[last_row:last_row + 1, :]
            a = (b_last - b_c) + li_c
            m_loc = jnp.max(a, axis=0, keepdims=True)
            w = jnp.exp(a - m_loc)
            kw = k.astype(F32) * w
            c_loc = _dot(kw.T.astype(BF16), v)
            n_loc = jnp.sum(kw, axis=0, keepdims=True)
            m_new = jnp.maximum(b_last + m_prev, m_loc)
            w_old = jnp.exp(b_last + m_prev - m_new)
            w_new = jnp.exp(m_loc - m_new)
            c_scr[d, h] = w_old * c_st + w_new * c_loc
            n_scr[sr:sr + 1, :] = w_old * n_st + w_new * n_loc
            m_scr[sr:sr + 1, :] = jnp.broadcast_to(m_new, (1, LANES))

    @pl.when(last[s] == 1)
    def _():
        cfin[...] = c_scr[...]
        nfin[...] = n_scr[...]
        mfin[...] = m_scr[...]


def _mlstm_scan(main, gates, c0, n0, m0, dims):
    t = main.shape[0]
    d_model = dims["D"]
    nh = MLSTM_HEADS
    dv = d_model // nh
    dk = dv // 2
    L = MLSTM_CHUNK
    assert dims["seq"] == L and dims["dec_seq"] % L == 0
    bp, bs = dims["Bp"], dims["Bs"]
    ncs = dims["dec_seq"] // L
    pblk = dims["P"] // L
    fblk = np.concatenate([np.arange(bp), pblk + np.arange(bs * ncs)]).astype(np.int32)
    bblk = np.concatenate([np.arange(bp),
                           pblk + (np.arange(bs)[:, None] * ncs + (ncs - 1 - np.arange(ncs))[None, :]).reshape(-1)]
                          ).astype(np.int32)
    seq = np.concatenate([np.arange(bp), bp + np.repeat(np.arange(bs), ncs)]).astype(np.int32)
    first = np.concatenate([np.ones(bp), (np.tile(np.arange(ncs), bs) == 0)]).astype(np.int32)
    last = np.concatenate([np.ones(bp), (np.tile(np.arange(ncs), bs) == ncs - 1)]).astype(np.int32)
    nseq = bp + bs
    nsteps = fblk.shape[0]
    qk = nh * dk
    vw = nh * dv
    kern = functools.partial(_mlstm_kernel, L=L, NH=nh, DK=dk, DV=dv)

    def fmap(cb):
        return lambda s, fb, bb, sq, fi, la: (fb[s], cb)

    def bmap(cb):
        return lambda s, fb, bb, sq, fi, la: (bb[s], cb)

    def smap(nd):
        return lambda s, fb, bb, sq, fi, la: (sq[s],) + (0,) * nd

    grid_spec = pltpu.PrefetchScalarGridSpec(
        num_scalar_prefetch=5,
        grid=(nsteps,),
        in_specs=[
            pl.BlockSpec((L, qk), fmap(0)), pl.BlockSpec((L, qk), fmap(1)),
            pl.BlockSpec((L, vw), fmap(2 * qk // vw)), pl.BlockSpec((L, LANES), fmap(0)),
            pl.BlockSpec((L, qk), bmap(0)), pl.BlockSpec((L, qk), bmap(1)),
            pl.BlockSpec((L, vw), bmap(2 * qk // vw)), pl.BlockSpec((L, LANES), bmap(0)),
            pl.BlockSpec((None, 2, nh, dk, dv), smap(4)),
            pl.BlockSpec((None, 2 * nh, dk), smap(2)),
            pl.BlockSpec((None, 2 * nh, LANES), smap(2)),
        ],
        out_specs=(
            pl.BlockSpec((L, vw), fmap(0)), pl.BlockSpec((L, vw), bmap(0)),
            pl.BlockSpec((None, 2, nh, dk, dv), smap(4)),
            pl.BlockSpec((None, 2 * nh, dk), smap(2)),
            pl.BlockSpec((None, 2 * nh, LANES), smap(2)),
        ),
        scratch_shapes=[pltpu.VMEM((2, nh, dk, dv), F32), pltpu.VMEM((2 * nh, dk), F32),
                        pltpu.VMEM((2 * nh, LANES), F32)],
    )
    return pl.pallas_call(
        kern,
        out_shape=(jax.ShapeDtypeStruct((t, vw), F32), jax.ShapeDtypeStruct((t, vw), F32),
                   jax.ShapeDtypeStruct((nseq, 2, nh, dk, dv), F32),
                   jax.ShapeDtypeStruct((nseq, 2 * nh, dk), F32),
                   jax.ShapeDtypeStruct((nseq, 2 * nh, LANES), F32)),
        grid_spec=grid_spec,
        compiler_params=_cparams(("arbitrary",)),
        name="mlstm_scan",
    )(jnp.asarray(fblk), jnp.asarray(bblk), jnp.asarray(seq), jnp.asarray(first), jnp.asarray(last),
      main, main, main, gates, main, main, main, gates, c0, n0, m0)


def _mlstm_out_kernel(x_ref, hf_ref, hb_ref, o_ref, ng_ref, w_ref, g1_ref, out_ref, *, NH, DV):
    parts = []
    for h in range(NH):
        sl = slice(h * DV, (h + 1) * DV)
        hs = hf_ref[:, sl] + hb_ref[:, sl]
        hs = hs * lax.rsqrt(jnp.mean(hs * hs, axis=-1, keepdims=True) + EPS)
        hs = hs * ng_ref[:, sl]
        parts.append((hs * jax.nn.sigmoid(o_ref[:, sl].astype(F32))).astype(BF16))
    z = jnp.concatenate(parts, axis=1)
    out_ref[...] = x_ref[...] + g1_ref[...] * _dot(z, w_ref[...])


def _mlstm_out(x, hf, hb, main, norm_g, w_out, mods, layer, dims):
    t, d = x.shape
    tm = 256
    vw = hf.shape[1]
    kern = functools.partial(_mlstm_out_kernel, NH=MLSTM_HEADS, DV=vw // MLSTM_HEADS)
    ocol = (main.shape[1] - vw) // vw
    return pl.pallas_call(
        kern,
        out_shape=jax.ShapeDtypeStruct((t, d), F32),
        grid=(t // tm,),
        in_specs=[
            pl.BlockSpec((tm, d), lambda i: (i, 0)),
            pl.BlockSpec((tm, vw), lambda i: (i, 0)),
            pl.BlockSpec((tm, vw), lambda i: (i, 0)),
            pl.BlockSpec((tm, vw), lambda i: (i, ocol)),
            pl.BlockSpec((1, vw), lambda i: (0, 0)),
            pl.BlockSpec((vw, d), lambda i: (0, 0)),
            _mod_spec(layer, 2, tm, dims["P"], dims["dec_seq"], d),
        ],
        out_specs=pl.BlockSpec((tm, d), lambda i: (i, 0)),
        compiler_params=_cparams(("parallel",)),
        name="mlstm_out",
    )(x, hf, hb, main, norm_g, w_out, mods)


def _attn_inproj_kernel(x_ref, g_ref, sh_ref, sc_ref, w_ref, qn_ref, kn_ref, cos_ref, sin_ref,
                        oq_ref, okv_ref, h_scr, *, nq_blocks, heads_per_block, DH):
    j = pl.program_id(1)

    @pl.when(j == 0)
    def _():
        h_scr[...] = _norm_mod(x_ref[...], g_ref[...], sh_ref[...], sc_ref[...]).astype(BF16)

    acc = _dot(h_scr[...], w_ref[...])

    def normed(gain):
        cos = cos_ref[...]
        sin = sin_ref[...]
        lane = lax.broadcasted_iota(jnp.int32, cos.shape, 1)
        first_half = (lane % (DH // 2)) < (DH // 4)
        outs = []
        for hh in range(heads_per_block):
            a = acc[:, hh * DH:(hh + 1) * DH]
            r = a * lax.rsqrt(jnp.mean(a * a, axis=-1, keepdims=True) + EPS) * gain
            partner = jnp.where(first_half, pltpu.roll(r, DH - DH // 4, 1), pltpu.roll(r, DH // 4, 1))
            outs.append(r * cos + partner * sin)
        return jnp.concatenate(outs, axis=1)

    @pl.when(j < nq_blocks)
    def _():
        oq_ref[...] = normed(qn_ref[...]).astype(oq_ref.dtype)

    @pl.when(j == nq_blocks)
    def _():
        okv_ref[...] = normed(kn_ref[...])

    @pl.when(j > nq_blocks)
    def _():
        okv_ref[...] = acc


def _attn_inproj(x, norm_g, mods, layer, w_in, qn_g, kn_g, cos_t, sin_t, dims):
    t, d = x.shape
    n = w_in.shape[1]
    dh = d // ATTN_HEADS
    tm, tn = 1024, 512
    assert ATTN_KV_HEADS * dh == tn
    nqb = (ATTN_HEADS * dh) // tn
    kern = functools.partial(_attn_inproj_kernel, nq_blocks=nqb, heads_per_block=tn // dh, DH=dh)
    return pl.pallas_call(
        kern,
        out_shape=(jax.ShapeDtypeStruct((t, ATTN_HEADS * dh), BF16),
                   jax.ShapeDtypeStruct((t, 2 * ATTN_KV_HEADS * dh), F32)),
        grid=(t // tm, n // tn),
        in_specs=[
            pl.BlockSpec((tm, d), lambda i, j: (i, 0)),
            pl.BlockSpec((1, d), lambda i, j: (0, 0)),
            _mod_spec(layer, 0, tm, dims["P"], dims["dec_seq"], d, with_j=True),
            _mod_spec(layer, 1, tm, dims["P"], dims["dec_seq"], d, with_j=True),
            pl.BlockSpec((d, tn), lambda i, j: (0, j)),
            pl.BlockSpec((1, dh), lambda i, j: (0, 0)),
            pl.BlockSpec((1, dh), lambda i, j: (0, 0)),
            pl.BlockSpec((tm, dh), lambda i, j: (i, 0)),
            pl.BlockSpec((tm, dh), lambda i, j: (i, 0)),
        ],
        out_specs=(pl.BlockSpec((tm, tn), lambda i, j: (i, jnp.minimum(j, nqb - 1))),
                   pl.BlockSpec((tm, tn), lambda i, j: (i, jnp.maximum(j - nqb, 0)))),
        scratch_shapes=[pltpu.VMEM((tm, d), BF16)],
        compiler_params=_cparams(("parallel", "arbitrary")),
        name="attn_inproj",
    )(x, norm_g, mods, mods, w_in, qn_g, kn_g, cos_t, sin_t)


def _softmax_sink_rows(s_list, sink_col):
    m = sink_col
    for s in s_list:
        m = jnp.maximum(m, jnp.max(s, axis=1, keepdims=True))
    ps = [jnp.exp(s - m) for s in s_list]
    tot = jnp.exp(sink_col - m)
    for p in ps:
        tot = tot + jnp.sum(p, axis=1, keepdims=True)
    return [p / tot for p in ps]


def _ctx_attn_kernel(sink_ref, q_ref, k_ref, v_ref, o_ref, *, G, DH):
    kv = pl.program_id(1)
    k = k_ref[...].astype(BF16)
    v = v_ref[...].astype(BF16)
    scale = float(DH) ** -0.5
    for g in range(G):
        q = q_ref[:, g * DH:(g + 1) * DH]
        s = _dot_nt(q, k) * scale
        sink = jnp.full((s.shape[0], 1), sink_ref[kv * G + g], F32)
        (p,) = _softmax_sink_rows([s], sink)
        o_ref[:, g * DH:(g + 1) * DH] = _dot(p.astype(BF16), v).astype(o_ref.dtype)


def _lat_attn_kernel(sink_ref, q_ref, kp_ref, kc_ref, kn_ref, vp_ref, vc_ref, vn_ref, kx_ref, vx_ref, o_ref,
                     *, G, DH, NB, BLK):
    kv = pl.program_id(1)
    i = pl.program_id(2)
    scale = float(DH) ** -0.5
    q = jnp.concatenate([q_ref[:, g * DH:(g + 1) * DH] for g in range(G)], axis=0)
    kw = jnp.concatenate([kp_ref[...], kc_ref[...], kn_ref[...]], axis=0).astype(BF16)
    vw = jnp.concatenate([vp_ref[...], vc_ref[...], vn_ref[...]], axis=0).astype(BF16)
    kx = kx_ref[...].astype(BF16)
    vx = vx_ref[...].astype(BF16)
    rows = G * BLK
    r = lax.broadcasted_iota(jnp.int32, (rows, 3 * BLK), 0) % BLK
    c = lax.broadcasted_iota(jnp.int32, (rows, 3 * BLK), 1)
    valid = (jnp.abs(c - BLK - r) <= WINDOW) & ((c >= BLK) | (i > 0)) & ((c < 2 * BLK) | (i < NB - 1))
    s_w = jnp.where(valid, _dot_nt(q, kw) * scale, -jnp.inf)
    s_c = _dot_nt(q, kx) * scale
    rid = lax.broadcasted_iota(jnp.int32, (rows, 1), 0) // BLK
    sink = jnp.zeros((rows, 1), F32)
    for g in range(G):
        sink = jnp.where(rid == g, sink_ref[kv * G + g], sink)
    p_w, p_c = _softmax_sink_rows([s_w, s_c], sink)
    o = _dot(p_w.astype(BF16), vw) + _dot(p_c.astype(BF16), vx)
    for g in range(G):
        o_ref[:, g * DH:(g + 1) * DH] = o[g * BLK:(g + 1) * BLK, :].astype(o_ref.dtype)


def _attention(qh, kvh, cache_k, cache_v, sink, dims):
    t = qh.shape[0]
    dh = qh.shape[1] // ATTN_HEADS
    G, KV = ATTN_GROUP, ATTN_KV_HEADS
    bp, bs, seq, dec_seq = dims["Bp"], dims["Bs"], dims["seq"], dims["dec_seq"]
    gw = G * dh
    o_ctx = pl.pallas_call(
        functools.partial(_ctx_attn_kernel, G=G, DH=dh),
        out_shape=jax.ShapeDtypeStruct((dims["P"], ATTN_HEADS * dh), BF16),
        grid_spec=pltpu.PrefetchScalarGridSpec(
            num_scalar_prefetch=1,
            grid=(bp, KV),
            in_specs=[
                pl.BlockSpec((seq, gw), lambda b, kv, sk: (b, kv)),
                pl.BlockSpec((seq, dh), lambda b, kv, sk: (b, kv)),
                pl.BlockSpec((seq, dh), lambda b, kv, sk: (b, KV + kv)),
            ],
            out_specs=pl.BlockSpec((seq, gw), lambda b, kv, sk: (b, kv)),
        ),
        compiler_params=_cparams(("parallel", "parallel")),
        name="ctx_attn",
    )(sink, qh, kvh, kvh)
    blk = ATTN_BLOCK
    assert blk == WINDOW and dec_seq % blk == 0 and dims["P"] % blk == 0
    nb = dec_seq // blk
    off = dims["P"] // blk

    def qmap(r, kv, i, sk):
        return (off + r * nb + i, kv)

    def omap(r, kv, i, sk):
        return (r * nb + i, kv)

    def kmap(delta, colbase):
        def f(r, kv, i, sk):
            return (off + r * nb + jnp.clip(i + delta, 0, nb - 1), colbase + kv)
        return f

    o_lat = pl.pallas_call(
        functools.partial(_lat_attn_kernel, G=G, DH=dh, NB=nb, BLK=blk),
        out_shape=jax.ShapeDtypeStruct((dims["S"], ATTN_HEADS * dh), BF16),
        grid_spec=pltpu.PrefetchScalarGridSpec(
            num_scalar_prefetch=1,
            grid=(bs, KV, nb),
            in_specs=[
                pl.BlockSpec((blk, gw), qmap),
                pl.BlockSpec((blk, dh), kmap(-1, 0)), pl.BlockSpec((blk, dh), kmap(0, 0)),
                pl.BlockSpec((blk, dh), kmap(1, 0)),
                pl.BlockSpec((blk, dh), kmap(-1, KV)), pl.BlockSpec((blk, dh), kmap(0, KV)),
                pl.BlockSpec((blk, dh), kmap(1, KV)),
                pl.BlockSpec((None, None, cache_k.shape[2], dh), lambda r, kv, i, sk: (r, kv, 0, 0)),
                pl.BlockSpec((None, None, cache_v.shape[2], dh), lambda r, kv, i, sk: (r, kv, 0, 0)),
            ],
            out_specs=pl.BlockSpec((blk, gw), omap),
        ),
        compiler_params=_cparams(("parallel", "parallel", "arbitrary")),
        name="lat_attn",
    )(sink, qh, kvh, kvh, kvh, kvh, kvh, kvh, cache_k, cache_v)
    return jnp.concatenate([o_ctx, o_lat], axis=0)


def _proj_res_kernel(x_ref, a_ref, w_ref, g1_ref, out_ref):
    out_ref[...] = x_ref[...] + g1_ref[...] * _dot(a_ref[...], w_ref[...])


def _proj_res(x, a, w, mods, layer, which, dims):
    t, d = x.shape
    tm = 512
    return pl.pallas_call(
        _proj_res_kernel,
        out_shape=jax.ShapeDtypeStruct((t, d), F32),
        grid=(t // tm,),
        in_specs=[
            pl.BlockSpec((tm, d), lambda i: (i, 0)),
            pl.BlockSpec((tm, a.shape[1]), lambda i: (i, 0)),
            pl.BlockSpec(w.shape, lambda i: (0, 0)),
            _mod_spec(layer, which, tm, dims["P"], dims["dec_seq"], d),
        ],
        out_specs=pl.BlockSpec((tm, d), lambda i: (i, 0)),
        compiler_params=_cparams(("parallel",)),
        name="proj_res",
    )(x, a, w, mods)


def _h2_router_kernel(x_ref, g_ref, sh_ref, sc_ref, wr_ref, h_ref, lg_ref):
    h = _norm_mod(x_ref[...], g_ref[...], sh_ref[...], sc_ref[...])
    h_ref[...] = h
    lg_ref[...] = _dot(h.astype(BF16), wr_ref[...])


def _h2_router(x, norm_g, mods, layer, w_router_pad, dims):
    t, d = x.shape
    tm = 512
    return pl.pallas_call(
        _h2_router_kernel,
        out_shape=(jax.ShapeDtypeStruct((t, d), F32), jax.ShapeDtypeStruct((t, LANES), F32)),
        grid=(t // tm,),
        in_specs=[
            pl.BlockSpec((tm, d), lambda i: (i, 0)),
            pl.BlockSpec((1, d), lambda i: (0, 0)),
            _mod_spec(layer, 3, tm, dims["P"], dims["dec_seq"], d),
            _mod_spec(layer, 4, tm, dims["P"], dims["dec_seq"], d),
            pl.BlockSpec((d, LANES), lambda i: (0, 0)),
        ],
        out_specs=(pl.BlockSpec((tm, d), lambda i: (i, 0)), pl.BlockSpec((tm, LANES), lambda i: (i, 0))),
        compiler_params=_cparams(("parallel",)),
        name="h2_router",
    )(x, norm_g, mods, mods, w_router_pad)


def _moe_kernel(idx_ref, h2_hbm, wg_ref, wu_ref, wd_ref, gate_ref, g2a_ref, g2b_ref, x_hbm, o_hbm,
                xf_ref, xb_ref, hm_ref, acc_ref, sem_x, sem_acc, sem_out, *, TM, TF, NF, MT, NTILES):
    del x_hbm
    s = pl.program_id(2)
    n = pl.program_id(0) * MT + pl.program_id(1)
    slot = n % 2
    base = n * TM
    chunk = TM // NF
    half = TM // 2

    def rows_wait(sem):
        pltpu.make_async_copy(h2_hbm.at[pl.ds(0, TM), :], xf_ref, sem).wait()

    def h2_copy(tile_base, r):
        tok = idx_ref[tile_base + r]
        return pltpu.make_async_copy(h2_hbm.at[pl.ds(tok, 1), :], xf_ref.at[pl.ds(r, 1), :], sem_x)

    def out_copy(tile_base, src_slot, r):
        tok = idx_ref[tile_base + r]
        return pltpu.make_async_copy(acc_ref.at[src_slot, pl.ds(r, 1), :], o_hbm.at[pl.ds(tok, 1), :], sem_out)

    def issue_h2(tile_base, r0, cnt):
        for r in range(cnt):
            h2_copy(tile_base, r0 + r).start()

    def issue_acc(r0, cnt):
        for r in range(cnt):
            tok = idx_ref[base + r0 + r]
            pltpu.make_async_copy(o_hbm.at[pl.ds(tok, 1), :], acc_ref.at[slot, pl.ds(r0 + r, 1), :], sem_acc).start()

    def issue_out(tile_base, src_slot, r0, cnt):
        for r in range(cnt):
            out_copy(tile_base, src_slot, r0 + r).start()

    @pl.when((s == 0) & (n == 0))
    def _():
        def body(r, carry):
            h2_copy(0, r).start()
            return carry
        lax.fori_loop(0, TM, body, 0)

    @pl.when((s == 0) & (n >= 2))
    def _():
        rows_wait(sem_out)

    @pl.when(s == 0)
    def _():
        rows_wait(sem_x)
        xb_ref[...] = xf_ref[...].astype(BF16)

    def hidden_block():
        xb = xb_ref[...]
        a = _dot(xb, wg_ref[...].astype(BF16))
        u = _dot(xb, wu_ref[...].astype(BF16))
        hm_ref[s] = (a * jax.nn.sigmoid(a) * u).astype(BF16)

    @pl.when((s < NF) & (n == 0))
    def _():
        issue_acc(s * chunk, chunk)
        hidden_block()

    @pl.when((s < NF) & (n > 0))
    def _():
        issue_acc(s * chunk, chunk)
        issue_out(base - TM, 1 - slot, s * chunk, chunk)
        hidden_block()

    @pl.when(s == NF)
    def _():
        rows_wait(sem_acc)

    for c in range(NF):
        @pl.when(s == NF + c)
        def _(c=c):
            issue_h2(base + TM, c * chunk, chunk)
            cols = slice(c * TF, (c + 1) * TF)
            y = _dot(hm_ref[0], wd_ref[0:TF, :].astype(BF16))
            for f in range(1, NF):
                y = y + _dot(hm_ref[f], wd_ref[f * TF:(f + 1) * TF, :].astype(BF16))
            y = y * gate_ref[...]
            acc_ref[slot, 0:half, cols] = acc_ref[slot, 0:half, cols] + y[0:half] * g2a_ref[:, cols]
            acc_ref[slot, half:TM, cols] = acc_ref[slot, half:TM, cols] + y[half:TM] * g2b_ref[:, cols]

    @pl.when((s == 2 * NF - 1) & (n == NTILES - 1))
    def _():
        def body(r, carry):
            out_copy(base, slot, r).start()
            return carry
        lax.fori_loop(0, TM, body, 0)
        rows_wait(sem_out)
        rows_wait(sem_out)
        rows_wait(sem_x)


def _moe_fused(x, h2, idx_flat, gates_col, w_gate, w_up, w_down, mods, layer, dims):
    t, d = x.shape
    e = N_EXPERTS
    dff = w_gate.shape[-1]
    rows = idx_flat.shape[0] // e
    rows_p = dims["Bp"] * (CAPACITY_FACTOR * dims["seq"] // e)
    cap_s = CAPACITY_FACTOR * dims["dec_seq"] // e
    half = cap_s
    tm = 2 * half
    assert rows_p % tm == 0 and (rows - rows_p) % tm == 0
    mt = rows // tm
    tf = 256
    nf = dff // tf
    assert d == dff and tm % nf == 0 and mt >= 2
    ntiles = e * mt
    idx_pad = jnp.concatenate([idx_flat, jnp.zeros((tm,), jnp.int32)])

    def grp(m, h):
        r = m * tm + h * half
        return jnp.where(r < rows_p, 0, 1 + (r - rows_p) // cap_s)

    kern = functools.partial(_moe_kernel, TM=tm, TF=tf, NF=nf, MT=mt, NTILES=ntiles)
    return pl.pallas_call(
        kern,
        out_shape=jax.ShapeDtypeStruct((t, d), F32),
        grid_spec=pltpu.PrefetchScalarGridSpec(
            num_scalar_prefetch=1,
            grid=(e, mt, 2 * nf),
            in_specs=[
                pl.BlockSpec(memory_space=pl.ANY),
                pl.BlockSpec((None, None, d, tf), lambda ex, m, s, idx: (layer, ex, 0, jnp.minimum(s, nf - 1))),
                pl.BlockSpec((None, None, d, tf), lambda ex, m, s, idx: (layer, ex, 0, jnp.minimum(s, nf - 1))),
                pl.BlockSpec((None, None, dff, tf), lambda ex, m, s, idx: (layer, ex, 0, jnp.maximum(s - nf, 0))),
                pl.BlockSpec((tm, 1), lambda ex, m, s, idx: (ex * mt + m, 0)),
                pl.BlockSpec((None, None, None, 1, d), lambda ex, m, s, idx: (layer, grp(m, 0), 5, 0, 0)),
                pl.BlockSpec((None, None, None, 1, d), lambda ex, m, s, idx: (layer, grp(m, 1), 5, 0, 0)),
                pl.BlockSpec(memory_space=pl.ANY),
            ],
            out_specs=pl.BlockSpec(memory_space=pl.ANY),
            scratch_shapes=[
                pltpu.VMEM((tm, d), F32), pltpu.VMEM((tm, d), BF16), pltpu.VMEM((nf, tm, tf), BF16),
                pltpu.VMEM((2, tm, d), F32),
                pltpu.SemaphoreType.DMA, pltpu.SemaphoreType.DMA, pltpu.SemaphoreType.DMA,
            ],
        ),
        input_output_aliases={8: 0},
        compiler_params=_cparams(("arbitrary", "arbitrary", "arbitrary")),
        name="moe_fused",
    )(idx_pad, h2, w_gate, w_up, w_down, gates_col, mods, mods, x)


def _route(logits, dims):
    e = N_EXPERTS
    bp, bs, seq, dec_seq, p = dims["Bp"], dims["Bs"], dims["seq"], dims["dec_seq"], dims["P"]
    aff = jax.nn.softmax(logits[:, :e], axis=-1)
    cap_p = CAPACITY_FACTOR * seq // e
    cap_s = CAPACITY_FACTOR * dec_seq // e
    gp, ip = lax.top_k(jnp.swapaxes(aff[:p].reshape(bp, seq, e), 1, 2), cap_p)
    gs, is_ = lax.top_k(jnp.swapaxes(aff[p:].reshape(bs, dec_seq, e), 1, 2), cap_s)
    ip = ip + (jnp.arange(bp, dtype=jnp.int32) * seq)[:, None, None]
    is_ = is_ + (p + jnp.arange(bs, dtype=jnp.int32) * dec_seq)[:, None, None]
    idx = jnp.concatenate([jnp.swapaxes(ip, 0, 1).reshape(e, bp * cap_p),
                           jnp.swapaxes(is_, 0, 1).reshape(e, bs * cap_s)], axis=1)
    gate = jnp.concatenate([jnp.swapaxes(gp, 0, 1).reshape(e, bp * cap_p),
                            jnp.swapaxes(gs, 0, 1).reshape(e, bs * cap_s)], axis=1)
    return idx.astype(jnp.int32), gate


def _moe(x, norm_g, mods, layer, w_router_pad, w_gate, w_up, w_down, dims):
    e = N_EXPERTS
    h2, logits = _h2_router(x, norm_g, mods, layer, w_router_pad, dims)
    idx, gate = _route(logits, dims)
    return _moe_fused(x, h2, idx.reshape(-1), gate.reshape(-1, 1), w_gate, w_up, w_down, mods, layer, dims)


def _rope_tables(dims, dh):
    n, p = dims["dec_seq"], dims["P"]
    axis_dim = dh // 2
    tpos = jnp.arange(n)
    rowp = (tpos // GRID_W).astype(F32)
    colp = (tpos % GRID_W).astype(F32)
    inv = ROPE_BASE ** (-jnp.arange(0, axis_dim, 2, dtype=F32) / axis_dim)
    ar = rowp[:, None] * inv
    ac = colp[:, None] * inv
    cos = jnp.concatenate([jnp.cos(ar), jnp.cos(ar), jnp.cos(ac), jnp.cos(ac)], axis=1)
    sin = jnp.concatenate([-jnp.sin(ar), jnp.sin(ar), -jnp.sin(ac), jnp.sin(ac)], axis=1)
    cos = jnp.concatenate([jnp.ones((p, dh), F32), jnp.tile(cos, (dims["Bs"], 1))], axis=0)
    sin = jnp.concatenate([jnp.zeros((p, dh), F32), jnp.tile(sin, (dims["Bs"], 1))], axis=0)
    return cos, sin


def kernel(x_prompt, x_sample, state_mlstm_C, state_mlstm_n, state_mlstm_m, cache_attn_k, cache_attn_v,
           c, c_ctx, w_mod, b_mod, norm1_g, norm2_g, m_w_in, m_b_gate, m_norm_g, m_w_out,
           a_w_in, a_q_norm_g, a_k_norm_g, a_sink, a_w_out, e_w_router, e_w_gate, e_w_up, e_w_down):
    bp, seq, d = x_prompt.shape
    bs, dec_seq, _ = x_sample.shape
    depth = w_mod.shape[0]
    dims = dict(Bp=bp, Bs=bs, seq=seq, dec_seq=dec_seq, P=bp * seq, S=bs * dec_seq, D=d)
    nh = MLSTM_HEADS
    dv = d // nh
    dk = dv // 2
    qk, vw = nh * dk, nh * dv
    dh = d // ATTN_HEADS

    x = jnp.concatenate([x_prompt.reshape(bp * seq, d), x_sample.reshape(bs * dec_seq, d)], axis=0)

    cvecs = jnp.concatenate([c_ctx[None, :], c, jnp.zeros((N_GROUPS_PAD - 1 - bs, d), F32)], axis=0)
    mods = _adaln(cvecs, w_mod, b_mod).reshape(depth, N_GROUPS_PAD, N_MOD, 1, d)
    cos_t, sin_t = _rope_tables(dims, dh)

    new_c, new_n, new_m, new_k, new_v = [], [], [], [], []
    for l in range(depth):
        j = l // 2
        if l % 2 == 0:
            w_in = m_w_in[j]
            w_main = w_in[:, :2 * qk + 2 * vw].astype(BF16)
            w_gate = jnp.pad(w_in[:, 2 * qk + 2 * vw:], ((0, 0), (0, LANES - 4 * nh))).astype(BF16)
            b_gate = jnp.pad(m_b_gate[j], (0, LANES - 4 * nh)).reshape(1, LANES)
            main, gates = _mlstm_inproj(x, norm1_g[l].reshape(1, d), mods, l, w_main, w_gate, b_gate, dims)
            c0 = jnp.concatenate([jnp.zeros((bp, 2, nh, dk, dv), F32), state_mlstm_C[:, j]], axis=0)
            n0 = jnp.concatenate([jnp.zeros((bp, 2 * nh, dk), F32),
                                  state_mlstm_n[:, j].reshape(bs, 2 * nh, dk)], axis=0)
            m0 = jnp.concatenate([jnp.zeros((bp, 2 * nh), F32), state_mlstm_m[:, j].reshape(bs, 2 * nh)], axis=0)
            m0 = jnp.broadcast_to(m0[:, :, None], (bp + bs, 2 * nh, LANES))
            hf, hb, cfin, nfin, mfin = _mlstm_scan(main, gates, c0, n0, m0, dims)
            new_c.append(cfin[:bp])
            new_n.append(nfin[:bp].reshape(bp, 2, nh, dk))
            new_m.append(mfin[:bp, :, 0].reshape(bp, 2, nh))
            x = _mlstm_out(x, hf, hb, main, m_norm_g[j].reshape(1, vw), m_w_out[j].astype(BF16), mods, l, dims)
        else:
            qh, kvh = _attn_inproj(x, norm1_g[l].reshape(1, d), mods, l, a_w_in[j].astype(BF16),
                                   a_q_norm_g[j].reshape(1, dh), a_k_norm_g[j].reshape(1, dh), cos_t, sin_t, dims)
            kvw = ATTN_KV_HEADS * dh
            new_k.append(kvh[:bp * seq, :kvw].reshape(bp, seq, ATTN_KV_HEADS, dh).transpose(0, 2, 1, 3))
            new_v.append(kvh[:bp * seq, kvw:].reshape(bp, seq, ATTN_KV_HEADS, dh).transpose(0, 2, 1, 3))
            o = _attention(qh, kvh, cache_attn_k[:, j], cache_attn_v[:, j], a_sink[j], dims)
            x = _proj_res(x, o, a_w_out[j].astype(BF16), mods, l, 2, dims)
        w_router_pad = jnp.pad(e_w_router[l], ((0, 0), (0, LANES - N_EXPERTS))).astype(BF16)
        x = _moe(x, norm2_g[l].reshape(1, d), mods, l, w_router_pad, e_w_gate, e_w_up, e_w_down, dims)

    p = bp * seq
    return (x[:p].reshape(bp, seq, d), x[p:].reshape(bs, dec_seq, d),
            jnp.stack(new_c, axis=1), jnp.stack(new_n, axis=1), jnp.stack(new_m, axis=1),
            jnp.stack(new_k, axis=1), jnp.stack(new_v, axis=1))
```

```python
import functools

import jax
import jax.numpy as jnp
import numpy as np
from jax import lax
from jax.experimental import pallas as pl
from jax.experimental.pallas import tpu as pltpu

F32 = jnp.float32
BF16 = jnp.bfloat16

GRID_W = 64
MLSTM_HEADS = 8
ATTN_HEADS = 16
ATTN_KV_HEADS = 4
ATTN_GROUP = ATTN_HEADS // ATTN_KV_HEADS
WINDOW = 128
ROPE_BASE = 10000.0
N_EXPERTS = 16
CAPACITY_FACTOR = 2
N_MOD = 6
EPS = 1e-6
N_GROUPS_PAD = 8
LANES = 128
MLSTM_CHUNK = 256
ATTN_BLOCK = 128
VMEM_LIMIT = 56 * 1024 * 1024


def _cparams(sem):
    return pltpu.CompilerParams(dimension_semantics=sem, vmem_limit_bytes=VMEM_LIMIT)


def _dot(a, b):
    return jnp.dot(a, b, preferred_element_type=F32)


def _dot_nt(a, b):
    return lax.dot_general(a, b, (((1,), (1,)), ((), ())), preferred_element_type=F32)


def _group_of_block(i, rows_per_block, n_prompt, dec_seq):
    row = i * rows_per_block
    return jnp.where(row < n_prompt, 0, 1 + (row - n_prompt) // dec_seq)


def _norm_mod(x, g, shift, scale):
    y = x * lax.rsqrt(jnp.mean(x * x, axis=-1, keepdims=True) + EPS)
    return (y * g) * (1.0 + scale) + shift


def _log_sigmoid(x):
    return jnp.minimum(x, 0.0) - jnp.log(1.0 + jnp.exp(-jnp.abs(x)))


def _split3(x):
    hi = x.astype(BF16)
    r1 = x - hi.astype(F32)
    mid = r1.astype(BF16)
    lo = (r1 - mid.astype(F32)).astype(BF16)
    return hi, mid, lo


def _adaln_kernel(c_ref, w_ref, b_ref, o_ref):
    c = c_ref[...]
    s = (c * jax.nn.sigmoid(c)).astype(BF16)
    o_ref[...] = _dot(s, w_ref[...].astype(BF16)) + b_ref[...]


def _adaln(cvecs, w_mod, b_mod):
    depth, d, n = w_mod.shape
    tn = 1024
    return pl.pallas_call(
        _adaln_kernel,
        out_shape=jax.ShapeDtypeStruct((depth, N_GROUPS_PAD, n), F32),
        grid=(depth, n // tn),
        in_specs=[
            pl.BlockSpec((N_GROUPS_PAD, d), lambda l, j: (0, 0)),
            pl.BlockSpec((None, d, tn), lambda l, j: (l, 0, j)),
            pl.BlockSpec((None, 1, tn), lambda l, j: (l, 0, j)),
        ],
        out_specs=pl.BlockSpec((None, N_GROUPS_PAD, tn), lambda l, j: (l, 0, j)),
        compiler_params=_cparams(("parallel", "parallel")),
        name="adaln",
    )(cvecs, w_mod, b_mod.reshape(depth, 1, n))


def _mod_spec(layer, which, tm, n_prompt, dec_seq, d, ncols=None, with_j=False):
    if ncols is None:
        ncols = d
    if with_j:
        return pl.BlockSpec((None, None, None, 1, ncols),
                            lambda i, j: (layer, _group_of_block(i, tm, n_prompt, dec_seq), which, 0, 0))
    return pl.BlockSpec((None, None, None, 1, ncols),
                        lambda i: (layer, _group_of_block(i, tm, n_prompt, dec_seq), which, 0, 0))


def _mlstm_inproj_kernel(x_ref, g_ref, sh_ref, sc_ref, w_ref, wg_ref, bg_ref, o_ref, og_ref, h_scr,
                         *, nq_blocks, qscale):
    j = pl.program_id(1)

    @pl.when(j == 0)
    def _():
        h = _norm_mod(x_ref[...], g_ref[...], sh_ref[...], sc_ref[...]).astype(BF16)
        h_scr[...] = h
        og_ref[...] = _dot(h, wg_ref[...]) + bg_ref[...]

    acc = _dot(h_scr[...], w_ref[...])
    scale = jnp.where(j < nq_blocks, qscale, 1.0).astype(F32)
    o_ref[...] = (acc * scale).astype(o_ref.dtype)


def _mlstm_inproj(x, norm_g, mods, layer, w_main, w_gate, b_gate, dims):
    t, d = x.shape
    n = w_main.shape[1]
    tm, tn = 1024, 512
    dk = d // MLSTM_HEADS // 2
    kern = functools.partial(_mlstm_inproj_kernel, nq_blocks=(MLSTM_HEADS * dk) // tn, qscale=float(dk) ** -0.5)
    return pl.pallas_call(
        kern,
        out_shape=(jax.ShapeDtypeStruct((t, n), BF16), jax.ShapeDtypeStruct((t, LANES), F32)),
        grid=(t // tm, n // tn),
        in_specs=[
            pl.BlockSpec((tm, d), lambda i, j: (i, 0)),
            pl.BlockSpec((1, d), lambda i, j: (0, 0)),
            _mod_spec(layer, 0, tm, dims["P"], dims["dec_seq"], d, with_j=True),
            _mod_spec(layer, 1, tm, dims["P"], dims["dec_seq"], d, with_j=True),
            pl.BlockSpec((d, tn), lambda i, j: (0, j)),
            pl.BlockSpec((d, LANES), lambda i, j: (0, 0)),
            pl.BlockSpec((1, LANES), lambda i, j: (0, 0)),
        ],
        out_specs=(pl.BlockSpec((tm, tn), lambda i, j: (i, j)),
                   pl.BlockSpec((tm, LANES), lambda i, j: (i, 0))),
        scratch_shapes=[pltpu.VMEM((tm, d), BF16)],
        compiler_params=_cparams(("parallel", "arbitrary")),
        name="mlstm_inproj",
    )(x, norm_g, mods, mods, w_main, w_gate, b_gate)


def _mlstm_kernel(fblk, bblk, seq, first, last,
                  qf, kf, vf, gf, qb, kb, vb, gb, c0, n0, m0,
                  hf, hb, cfin, nfin, mfin, c_scr, n_scr, m_scr, *, L, NH, DK, DV):
    s = pl.program_id(0)

    @pl.when(first[s] == 1)
    def _():
        c_scr[...] = c0[...]
        n_scr[...] = n0[...]
        m_scr[...] = m0[...]

    row = lax.broadcasted_iota(jnp.int32, (L, L), 0)
    col = lax.broadcasted_iota(jnp.int32, (L, L), 1)
    tril = col <= row
    triu = col >= row
    tril_bf = tril.astype(BF16)
    triu_bf = triu.astype(BF16)

    dirs = ((qf, kf, vf, gf, hf, tril, tril_bf, triu_bf, L - 1),
            (qb, kb, vb, gb, hb, triu, triu_bf, tril_bf, 0))
    for d, (q_ref, k_ref, v_ref, g_ref, h_ref, keep, keep_bf, keep_t_bf, last_row) in enumerate(dirs):
        g = g_ref[...]
        lf = _log_sigmoid(g)
        l1, l2, l3 = _split3(lf)
        bcol = _dot(keep_bf, l1) + _dot(keep_bf, l2) + _dot(keep_bf, l3)
        g_t = g.T
        lf_t = _log_sigmoid(g_t)
        t1, t2, t3 = _split3(lf_t)
        brow = _dot(t1, keep_t_bf) + _dot(t2, keep_t_bf) + _dot(t3, keep_t_bf)
        base = 2 * NH * d
        for h in range(NH):
            ci = base + h
            cf = base + NH + h
            sr = NH * d + h
            li_c = g[:, ci:ci + 1]
            b_c = bcol[:, cf:cf + 1]
            li_r = g_t[ci:ci + 1, :]
            b_r = brow[cf:cf + 1, :]
            m_prev = m_scr[sr:sr + 1, 0:1]
            dmat = jnp.where(keep, (b_c - b_r) + li_r, -jnp.inf)
            gv = b_c + m_prev
            m_t = jnp.maximum(gv, jnp.max(dmat, axis=1, keepdims=True))
            w_inter = jnp.exp(gv - m_t)
            w_intra = jnp.exp(dmat - m_t)
            q = q_ref[:, h * DK:(h + 1) * DK]
            k = k_ref[:, h * DK:(h + 1) * DK]
            v = v_ref[:, h * DV:(h + 1) * DV]
            sc = _dot_nt(q, k) * w_intra
            c_st = c_scr[d, h]
            n_st = n_scr[sr:sr + 1, :]
            num = w_inter * _dot(q, c_st.astype(BF16)) + _dot(sc.astype(BF16), v)
            den = (w_inter * jnp.sum(q.astype(F32) * n_st, axis=1, keepdims=True)
                   + jnp.sum(sc, axis=1, keepdims=True))
            h_ref[:, h * DV:(h + 1) * DV] = num / jnp.maximum(jnp.abs(den), jnp.exp(-m_t))
            b_last = b_c[last_row:last_row + 1, :]
            a = (b_last - b_c) + li_c
            m_loc = jnp.max(a, axis=0, keepdims=True)
            w = jnp.exp(a - m_loc)
            kw = k.astype(F32) * w
            c_loc = _dot(kw.T.astype(BF16), v)
            n_loc = jnp.sum(kw, axis=0, keepdims=True)
            m_new = jnp.maximum(b_last + m_prev, m_loc)
            w_old = jnp.exp(b_last + m_prev - m_new)
            w_new = jnp.exp(m_loc - m_new)
            c_scr[d, h] = w_old * c_st + w_new * c_loc
            n_scr[sr:sr + 1, :] = w_old * n_st + w_new * n_loc
            m_scr[sr:sr + 1, :] = jnp.broadcast_to(m_new, (1, LANES))

    @pl.when(last[s] == 1)
    def _():
        cfin[...] = c_scr[...]
        nfin[...] = n_scr[...]
        mfin[...] = m_scr[...]


def _mlstm_scan(main, gates, c0, n0, m0, dims):
    t = main.shape[0]
    d_model = dims["D"]
    nh = MLSTM_HEADS
    dv = d_model // nh
    dk = dv // 2
    L = MLSTM_CHUNK
    assert dims["seq"] == L and dims["dec_seq"] % L == 0
    bp, bs = dims["Bp"], dims["Bs"]
    ncs = dims["dec_seq"] // L
    pblk = dims["P"] // L
    fblk = np.concatenate([np.arange(bp), pblk + np.arange(bs * ncs)]).astype(np.int32)
    bblk = np.concatenate([np.arange(bp),
                           pblk + (np.arange(bs)[:, None] * ncs + (ncs - 1 - np.arange(ncs))[None, :]).reshape(-1)]
                          ).astype(np.int32)
    seq = np.concatenate([np.arange(bp), bp + np.repeat(np.arange(bs), ncs)]).astype(np.int32)
    first = np.concatenate([np.ones(bp), (np.tile(np.arange(ncs), bs) == 0)]).astype(np.int32)
    last = np.concatenate([np.ones(bp), (np.tile(np.arange(ncs), bs) == ncs - 1)]).astype(np.int32)
    nseq = bp + bs
    nsteps = fblk.shape[0]
    qk = nh * dk
    vw = nh * dv
    kern = functools.partial(_mlstm_kernel, L=L, NH=nh, DK=dk, DV=dv)

    def fmap(cb):
        return lambda s, fb, bb, sq, fi, la: (fb[s], cb)

    def bmap(cb):
        return lambda s, fb, bb, sq, fi, la: (bb[s], cb)

    def smap(nd):
        return lambda s, fb, bb, sq, fi, la: (sq[s],) + (0,) * nd

    grid_spec = pltpu.PrefetchScalarGridSpec(
        num_scalar_prefetch=5,
        grid=(nsteps,),
        in_specs=[
            pl.BlockSpec((L, qk), fmap(0)), pl.BlockSpec((L, qk), fmap(1)),
            pl.BlockSpec((L, vw), fmap(2 * qk // vw)), pl.BlockSpec((L, LANES), fmap(0)),
            pl.BlockSpec((L, qk), bmap(0)), pl.BlockSpec((L, qk), bmap(1)),
            pl.BlockSpec((L, vw), bmap(2 * qk // vw)), pl.BlockSpec((L, LANES), bmap(0)),
            pl.BlockSpec((None, 2, nh, dk, dv), smap(4)),
            pl.BlockSpec((None, 2 * nh, dk), smap(2)),
            pl.BlockSpec((None, 2 * nh, LANES), smap(2)),
        ],
        out_specs=(
            pl.BlockSpec((L, vw), fmap(0)), pl.BlockSpec((L, vw), bmap(0)),
            pl.BlockSpec((None, 2, nh, dk, dv), smap(4)),
            pl.BlockSpec((None, 2 * nh, dk), smap(2)),
            pl.BlockSpec((None, 2 * nh, LANES), smap(2)),
        ),
        scratch_shapes=[pltpu.VMEM((2, nh, dk, dv), F32), pltpu.VMEM((2 * nh, dk), F32),
                        pltpu.VMEM((2 * nh, LANES), F32)],
    )
    return pl.pallas_call(
        kern,
        out_shape=(jax.ShapeDtypeStruct((t, vw), F32), jax.ShapeDtypeStruct((t, vw), F32),
                   jax.ShapeDtypeStruct((nseq, 2, nh, dk, dv), F32),
                   jax.ShapeDtypeStruct((nseq, 2 * nh, dk), F32),
                   jax.ShapeDtypeStruct((nseq, 2 * nh, LANES), F32)),
        grid_spec=grid_spec,
        compiler_params=_cparams(("arbitrary",)),
        name="mlstm_scan",
    )(jnp.asarray(fblk), jnp.asarray(bblk), jnp.asarray(seq), jnp.asarray(first), jnp.asarray(last),
      main, main, main, gates, main, main, main, gates, c0, n0, m0)


def _mlstm_out_kernel(x_ref, hf_ref, hb_ref, o_ref, ng_ref, w_ref, g1_ref, out_ref, *, NH, DV):
    parts = []
    for h in range(NH):
        sl = slice(h * DV, (h + 1) * DV)
        hs = hf_ref[:, sl] + hb_ref[:, sl]
        hs = hs * lax.rsqrt(jnp.mean(hs * hs, axis=-1, keepdims=True) + EPS)
        hs = hs * ng_ref[:, sl]
        parts.append((hs * jax.nn.sigmoid(o_ref[:, sl].astype(F32))).astype(BF16))
    z = jnp.concatenate(parts, axis=1)
    out_ref[...] = x_ref[...] + g1_ref[...] * _dot(z, w_ref[...])


def _mlstm_out(x, hf, hb, main, norm_g, w_out, mods, layer, dims):
    t, d = x.shape
    tm = 256
    vw = hf.shape[1]
    kern = functools.partial(_mlstm_out_kernel, NH=MLSTM_HEADS, DV=vw // MLSTM_HEADS)
    ocol = (main.shape[1] - vw) // vw
    return pl.pallas_call(
        kern,
        out_shape=jax.ShapeDtypeStruct((t, d), F32),
        grid=(t // tm,),
        in_specs=[
            pl.BlockSpec((tm, d), lambda i: (i, 0)),
            pl.BlockSpec((tm, vw), lambda i: (i, 0)),
            pl.BlockSpec((tm, vw), lambda i: (i, 0)),
            pl.BlockSpec((tm, vw), lambda i: (i, ocol)),
            pl.BlockSpec((1, vw), lambda i: (0, 0)),
            pl.BlockSpec((vw, d), lambda i: (0, 0)),
            _mod_spec(layer, 2, tm, dims["P"], dims["dec_seq"], d),
        ],
        out_specs=pl.BlockSpec((tm, d), lambda i: (i, 0)),
        compiler_params=_cparams(("parallel",)),
        name="mlstm_out",
    )(x, hf, hb, main, norm_g, w_out, mods)


def _attn_inproj_kernel(x_ref, g_ref, sh_ref, sc_ref, w_ref, qn_ref, kn_ref, cos_ref, sin_ref,
                        oq_ref, okv_ref, h_scr, *, nq_blocks, heads_per_block, DH):
    j = pl.program_id(1)

    @pl.when(j == 0)
    def _():
        h_scr[...] = _norm_mod(x_ref[...], g_ref[...], sh_ref[...], sc_ref[...]).astype(BF16)

    acc = _dot(h_scr[...], w_ref[...])

    def normed(gain):
        cos = cos_ref[...]
        sin = sin_ref[...]
        lane = lax.broadcasted_iota(jnp.int32, cos.shape, 1)
        first_half = (lane % (DH // 2)) < (DH // 4)
        outs = []
        for hh in range(heads_per_block):
            a = acc[:, hh * DH:(hh + 1) * DH]
            r = a * lax.rsqrt(jnp.mean(a * a, axis=-1, keepdims=True) + EPS) * gain
            partner = jnp.where(first_half, pltpu.roll(r, DH - DH // 4, 1), pltpu.roll(r, DH // 4, 1))
            outs.append(r * cos + partner * sin)
        return jnp.concatenate(outs, axis=1)

    @pl.when(j < nq_blocks)
    def _():
        oq_ref[...] = normed(qn_ref[...]).astype(oq_ref.dtype)

    @pl.when(j == nq_blocks)
    def _():
        okv_ref[...] = normed(kn_ref[...])

    @pl.when(j > nq_blocks)
    def _():
        okv_ref[...] = acc


def _attn_inproj(x, norm_g, mods, layer, w_in, qn_g, kn_g, cos_t, sin_t, dims):
    t, d = x.shape
    n = w_in.shape[1]
    dh = d // ATTN_HEADS
    tm, tn = 1024, 512
    assert ATTN_KV_HEADS * dh == tn
    nqb = (ATTN_HEADS * dh) // tn
    kern = functools.partial(_attn_inproj_kernel, nq_blocks=nqb, heads_per_block=tn // dh, DH=dh)
    return pl.pallas_call(
        kern,
        out_shape=(jax.ShapeDtypeStruct((t, ATTN_HEADS * dh), BF16),
                   jax.ShapeDtypeStruct((t, 2 * ATTN_KV_HEADS * dh), F32)),
        grid=(t // tm, n // tn),
        in_specs=[
            pl.BlockSpec((tm, d), lambda i, j: (i, 0)),
            pl.BlockSpec((1, d), lambda i, j: (0, 0)),
            _mod_spec(layer, 0, tm, dims["P"], dims["dec_seq"], d, with_j=True),
            _mod_spec(layer, 1, tm, dims["P"], dims["dec_seq"], d, with_j=True),
            pl.BlockSpec((d, tn), lambda i, j: (0, j)),
            pl.BlockSpec((1, dh), lambda i, j: (0, 0)),
            pl.BlockSpec((1, dh), lambda i, j: (0, 0)),
            pl.BlockSpec((tm, dh), lambda i, j: (i, 0)),
            pl.BlockSpec((tm, dh), lambda i, j: (i, 0)),
        ],
        out_specs=(pl.BlockSpec((tm, tn), lambda i, j: (i, jnp.minimum(j, nqb - 1))),
                   pl.BlockSpec((tm, tn), lambda i, j: (i, jnp.maximum(j - nqb, 0)))),
        scratch_shapes=[pltpu.VMEM((tm, d), BF16)],
        compiler_params=_cparams(("parallel", "arbitrary")),
        name="attn_inproj",
    )(x, norm_g, mods, mods, w_in, qn_g, kn_g, cos_t, sin_t)


def _softmax_sink_rows(s_list, sink_col):
    m = sink_col
    for s in s_list:
        m = jnp.maximum(m, jnp.max(s, axis=1, keepdims=True))
    ps = [jnp.exp(s - m) for s in s_list]
    tot = jnp.exp(sink_col - m)
    for p in ps:
        tot = tot + jnp.sum(p, axis=1, keepdims=True)
    return ps, tot


def _ctx_attn_kernel(sink_ref, q_ref, k_ref, v_ref, o_ref, *, G, DH):
    kv = pl.program_id(1)
    k = k_ref[...].astype(BF16)
    v = v_ref[...].astype(BF16)
    scale = float(DH) ** -0.5
    for g in range(G):
        q = q_ref[:, g * DH:(g + 1) * DH]
        s = _dot_nt(q, k) * scale
        sink = jnp.full((s.shape[0], 1), sink_ref[kv * G + g], F32)
        (p,), tot = _softmax_sink_rows([s], sink)
        o_ref[:, g * DH:(g + 1) * DH] = (_dot(p.astype(BF16), v) / tot).astype(o_ref.dtype)


def _lat_attn_kernel(sink_ref, q_ref, kp_ref, kc_ref, kn_ref, vp_ref, vc_ref, vn_ref, kx_ref, vx_ref, o_ref,
                     *, G, DH, NB, BLK):
    kv = pl.program_id(1)
    i = pl.program_id(2)
    scale = float(DH) ** -0.5
    q = jnp.concatenate([q_ref[:, g * DH:(g + 1) * DH] for g in range(G)], axis=0)
    rows = G * BLK
    r = lax.broadcasted_iota(jnp.int32, (rows, BLK), 0) % BLK
    c = lax.broadcasted_iota(jnp.int32, (rows, BLK), 1)
    s_p = jnp.where((c >= r) & (i > 0), _dot_nt(q, kp_ref[...].astype(BF16)) * scale, -jnp.inf)
    s_m = _dot_nt(q, kc_ref[...].astype(BF16)) * scale
    s_n = jnp.where((c <= r) & (i < NB - 1), _dot_nt(q, kn_ref[...].astype(BF16)) * scale, -jnp.inf)
    s_x = _dot_nt(q, kx_ref[...].astype(BF16)) * scale
    rid = lax.broadcasted_iota(jnp.int32, (rows, 1), 0) // BLK
    sink = jnp.zeros((rows, 1), F32)
    for g in range(G):
        sink = jnp.where(rid == g, sink_ref[kv * G + g], sink)
    (p_p, p_m, p_n, p_x), tot = _softmax_sink_rows([s_p, s_m, s_n, s_x], sink)
    o = (_dot(p_p.astype(BF16), vp_ref[...].astype(BF16)) + _dot(p_m.astype(BF16), vc_ref[...].astype(BF16))
         + _dot(p_n.astype(BF16), vn_ref[...].astype(BF16)) + _dot(p_x.astype(BF16), vx_ref[...].astype(BF16))) / tot
    for g in range(G):
        o_ref[:, g * DH:(g + 1) * DH] = o[g * BLK:(g + 1) * BLK, :].astype(o_ref.dtype)


def _attention(qh, kvh, cache_k, cache_v, sink, dims):
    t = qh.shape[0]
    dh = qh.shape[1] // ATTN_HEADS
    G, KV = ATTN_GROUP, ATTN_KV_HEADS
    bp, bs, seq, dec_seq = dims["Bp"], dims["Bs"], dims["seq"], dims["dec_seq"]
    gw = G * dh
    o_ctx = pl.pallas_call(
        functools.partial(_ctx_attn_kernel, G=G, DH=dh),
        out_shape=jax.ShapeDtypeStruct((dims["P"], ATTN_HEADS * dh), BF16),
        grid_spec=pltpu.PrefetchScalarGridSpec(
            num_scalar_prefetch=1,
            grid=(bp, KV),
            in_specs=[
                pl.BlockSpec((seq, gw), lambda b, kv, sk: (b, kv)),
                pl.BlockSpec((seq, dh), lambda b, kv, sk: (b, kv)),
                pl.BlockSpec((seq, dh), lambda b, kv, sk: (b, KV + kv)),
            ],
            out_specs=pl.BlockSpec((seq, gw), lambda b, kv, sk: (b, kv)),
        ),
        compiler_params=_cparams(("parallel", "parallel")),
        name="ctx_attn",
    )(sink, qh, kvh, kvh)
    blk = ATTN_BLOCK
    assert blk == WINDOW and dec_seq % blk == 0 and dims["P"] % blk == 0
    nb = dec_seq // blk
    off = dims["P"] // blk

    def qmap(r, kv, i, sk):
        return (off + r * nb + i, kv)

    def omap(r, kv, i, sk):
        return (r * nb + i, kv)

    def kmap(delta, colbase):
        def f(r, kv, i, sk):
            return (off + r * nb + jnp.clip(i + delta, 0, nb - 1), colbase + kv)
        return f

    o_lat = pl.pallas_call(
        functools.partial(_lat_attn_kernel, G=G, DH=dh, NB=nb, BLK=blk),
        out_shape=jax.ShapeDtypeStruct((dims["S"], ATTN_HEADS * dh), BF16),
        grid_spec=pltpu.PrefetchScalarGridSpec(
            num_scalar_prefetch=1,
            grid=(bs, KV, nb),
            in_specs=[
                pl.BlockSpec((blk, gw), qmap),
                pl.BlockSpec((blk, dh), kmap(-1, 0)), pl.BlockSpec((blk, dh), kmap(0, 0)),
                pl.BlockSpec((blk, dh), kmap(1, 0)),
                pl.BlockSpec((blk, dh), kmap(-1, KV)), pl.BlockSpec((blk, dh), kmap(0, KV)),
                pl.BlockSpec((blk, dh), kmap(1, KV)),
                pl.BlockSpec((None, None, cache_k.shape[2], dh), lambda r, kv, i, sk: (r, kv, 0, 0)),
                pl.BlockSpec((None, None, cache_v.shape[2], dh), lambda r, kv, i, sk: (r, kv, 0, 0)),
            ],
            out_specs=pl.BlockSpec((blk, gw), omap),
        ),
        compiler_params=_cparams(("parallel", "parallel", "arbitrary")),
        name="lat_attn",
    )(sink, qh, kvh, kvh, kvh, kvh, kvh, kvh, cache_k, cache_v)
    return jnp.concatenate([o_ctx, o_lat], axis=0)


def _proj_res_kernel(x_ref, a_ref, w_ref, g1_ref, out_ref):
    out_ref[...] = x_ref[...] + g1_ref[...] * _dot(a_ref[...], w_ref[...])


def _proj_res(x, a, w, mods, layer, which, dims):
    t, d = x.shape
    tm = 512
    return pl.pallas_call(
        _proj_res_kernel,
        out_shape=jax.ShapeDtypeStruct((t, d), F32),
        grid=(t // tm,),
        in_specs=[
            pl.BlockSpec((tm, d), lambda i: (i, 0)),
            pl.BlockSpec((tm, a.shape[1]), lambda i: (i, 0)),
            pl.BlockSpec(w.shape, lambda i: (0, 0)),
            _mod_spec(layer, which, tm, dims["P"], dims["dec_seq"], d),
        ],
        out_specs=pl.BlockSpec((tm, d), lambda i: (i, 0)),
        compiler_params=_cparams(("parallel",)),
        name="proj_res",
    )(x, a, w, mods)


def _h2_router_kernel(x_ref, g_ref, sh_ref, sc_ref, wr_ref, h_ref, lg_ref):
    hb = _norm_mod(x_ref[...], g_ref[...], sh_ref[...], sc_ref[...]).astype(BF16)
    lg_ref[...] = _dot(hb, wr_ref[...])
    bits = pltpu.bitcast(hb.astype(F32), jnp.int32)
    half = bits.shape[1] // 2
    lo = bits[:, :half]
    h_ref[...] = (bits[:, half:] & jnp.int32(-65536)) | lax.shift_right_logical(lo, jnp.full(lo.shape, 16, jnp.int32))


def _h2_router(x, norm_g, mods, layer, w_router_pad, dims):
    t, d = x.shape
    tm = 512
    return pl.pallas_call(
        _h2_router_kernel,
        out_shape=(jax.ShapeDtypeStruct((t, d // 2), jnp.int32), jax.ShapeDtypeStruct((t, LANES), F32)),
        grid=(t // tm,),
        in_specs=[
            pl.BlockSpec((tm, d), lambda i: (i, 0)),
            pl.BlockSpec((1, d), lambda i: (0, 0)),
            _mod_spec(layer, 3, tm, dims["P"], dims["dec_seq"], d),
            _mod_spec(layer, 4, tm, dims["P"], dims["dec_seq"], d),
            pl.BlockSpec((d, LANES), lambda i: (0, 0)),
        ],
        out_specs=(pl.BlockSpec((tm, d // 2), lambda i: (i, 0)), pl.BlockSpec((tm, LANES), lambda i: (i, 0))),
        compiler_params=_cparams(("parallel",)),
        name="h2_router",
    )(x, norm_g, mods, mods, w_router_pad)


def _moe_kernel(idx_ref, h2_hbm, wg_ref, wu_ref, wd_ref, gate_ref, *rest, TM, TF, NF, MT, NTILES, NSUB):
    g2_refs = rest[:NSUB]
    x_hbm, o_hbm, xf_ref, xb_ref, hm_ref, acc_ref, sem_x, sem_acc, sem_out = rest[NSUB:]
    del x_hbm
    s = pl.program_id(2)
    n = pl.program_id(0) * MT + pl.program_id(1)
    base = n * TM
    nh = NF // 2
    chunk_h2 = TM // NF
    chunk_rw = TM // nh
    sub = TM // NSUB
    dhalf = xf_ref.shape[1]

    def h2_wait():
        pltpu.make_async_copy(h2_hbm.at[pl.ds(0, TM), :], xf_ref, sem_x).wait()

    def stream_wait(sem):
        pltpu.make_async_copy(o_hbm.at[pl.ds(0, TM), :], acc_ref, sem).wait()

    def h2_copy(tile_base, r):
        tok = idx_ref[tile_base + r]
        return pltpu.make_async_copy(h2_hbm.at[pl.ds(tok, 1), :], xf_ref.at[pl.ds(r, 1), :], sem_x)

    def out_copy(tile_base, r):
        tok = idx_ref[tile_base + r]
        return pltpu.make_async_copy(acc_ref.at[pl.ds(r, 1), :], o_hbm.at[pl.ds(tok, 1), :], sem_out)

    def issue_h2(tile_base, r0, cnt):
        for r in range(cnt):
            h2_copy(tile_base, r0 + r).start()

    def issue_acc(r0, cnt):
        for r in range(cnt):
            tok = idx_ref[base + r0 + r]
            pltpu.make_async_copy(o_hbm.at[pl.ds(tok, 1), :], acc_ref.at[pl.ds(r0 + r, 1), :], sem_acc).start()

    def issue_out(tile_base, r0, cnt):
        for r in range(cnt):
            out_copy(tile_base, r0 + r).start()

    @pl.when((s == 0) & (n == 0))
    def _():
        def body(r, carry):
            h2_copy(0, r).start()
            return carry
        lax.fori_loop(0, TM, body, 0)

    @pl.when(s == 0)
    def _():
        h2_wait()
        w = xf_ref[...]
        xb_ref[:, 0:dhalf] = pltpu.bitcast(w << 16, F32).astype(BF16)
        xb_ref[:, dhalf:2 * dhalf] = pltpu.bitcast(w & jnp.int32(-65536), F32).astype(BF16)

    def hidden_block(issue, r0, cnt):
        wg = wg_ref[...].astype(BF16)
        wu = wu_ref[...].astype(BF16)
        part = cnt // NSUB
        for k in range(NSUB):
            rs = slice(k * sub, (k + 1) * sub)
            if issue is not None:
                issue(r0 + k * part, part)
            xb = xb_ref[rs, :]
            a = _dot(xb, wg)
            u = _dot(xb, wu)
            hm_ref[s, rs, :] = (a * jax.nn.sigmoid(a) * u).astype(BF16)

    @pl.when((s < nh) & (n == 0))
    def _():
        hidden_block(None, 0, 0)

    @pl.when((s < nh) & (n > 0))
    def _():
        hidden_block(functools.partial(issue_out, base - TM), s * chunk_rw, chunk_rw)

    @pl.when((s == nh) & (n > 0))
    def _():
        stream_wait(sem_out)

    @pl.when((s >= nh) & (s < NF))
    def _():
        hidden_block(issue_acc, (s - nh) * chunk_rw, chunk_rw)

    @pl.when(s == NF)
    def _():
        stream_wait(sem_acc)

    for c in range(NF):
        @pl.when(s == NF + c)
        def _(c=c):
            cols = slice(c * TF, (c + 1) * TF)
            wd = [wd_ref[f * TF:(f + 1) * TF, :].astype(BF16) for f in range(NF)]
            part = chunk_h2 // NSUB
            for k in range(NSUB):
                rs = slice(k * sub, (k + 1) * sub)
                issue_h2(base + TM, c * chunk_h2 + k * part, part)
                y = _dot(hm_ref[0, rs, :], wd[0])
                for f in range(1, NF):
                    y = y + _dot(hm_ref[f, rs, :], wd[f])
                acc_ref[rs, cols] = acc_ref[rs, cols] + (y * gate_ref[rs, :]) * g2_refs[k][:, cols]

    @pl.when((s == 2 * NF - 1) & (n == NTILES - 1))
    def _():
        def body(r, carry):
            out_copy(base, r).start()
            return carry
        lax.fori_loop(0, TM, body, 0)
        stream_wait(sem_out)
        h2_wait()


def _moe_fused(x, h2, idx_flat, gates_col, w_gate, w_up, w_down, mods, layer, dims):
    t, d = x.shape
    e = N_EXPERTS
    dff = w_gate.shape[-1]
    rows = idx_flat.shape[0] // e
    rows_p = dims["Bp"] * (CAPACITY_FACTOR * dims["seq"] // e)
    cap_s = CAPACITY_FACTOR * dims["dec_seq"] // e
    mt = 2
    tm = rows // mt
    sub = cap_s
    nsub = tm // sub
    assert rows % mt == 0 and tm % sub == 0 and rows_p % sub == 0 and rows_p <= tm
    tf = 256
    nf = dff // tf
    assert d == dff and nf % 2 == 0 and tm % (nf * nsub) == 0
    ntiles = e * mt
    idx_pad = jnp.concatenate([idx_flat, jnp.zeros((tm,), jnp.int32)])

    def g2_spec(k):
        def imap(ex, m, s, idx):
            r = m * tm + k * sub
            return (layer, jnp.where(r < rows_p, 0, 1 + (r - rows_p) // cap_s), 5, 0, 0)
        return pl.BlockSpec((None, None, None, 1, d), imap)

    kern = functools.partial(_moe_kernel, TM=tm, TF=tf, NF=nf, MT=mt, NTILES=ntiles, NSUB=nsub)
    return pl.pallas_call(
        kern,
        out_shape=jax.ShapeDtypeStruct((t, d), F32),
        grid_spec=pltpu.PrefetchScalarGridSpec(
            num_scalar_prefetch=1,
            grid=(e, mt, 2 * nf),
            in_specs=[
                pl.BlockSpec(memory_space=pl.ANY),
                pl.BlockSpec((None, None, d, tf), lambda ex, m, s, idx: (layer, ex, 0, jnp.minimum(s, nf - 1))),
                pl.BlockSpec((None, None, d, tf), lambda ex, m, s, idx: (layer, ex, 0, jnp.minimum(s, nf - 1))),
                pl.BlockSpec((None, None, dff, tf), lambda ex, m, s, idx: (layer, ex, 0, jnp.maximum(s - nf, 0))),
                pl.BlockSpec((tm, 1), lambda ex, m, s, idx: (ex * mt + m, 0)),
            ] + [g2_spec(k) for k in range(nsub)] + [pl.BlockSpec(memory_space=pl.ANY)],
            out_specs=pl.BlockSpec(memory_space=pl.ANY),
            scratch_shapes=[
                pltpu.VMEM((tm, d // 2), jnp.int32), pltpu.VMEM((tm, d), BF16), pltpu.VMEM((nf, tm, tf), BF16),
                pltpu.VMEM((tm, d), F32),
                pltpu.SemaphoreType.DMA, pltpu.SemaphoreType.DMA, pltpu.SemaphoreType.DMA,
            ],
        ),
        input_output_aliases={6 + nsub: 0},
        compiler_params=_cparams(("arbitrary", "arbitrary", "arbitrary")),
        name="moe_fused",
    )(idx_pad, h2, w_gate, w_up, w_down, gates_col, *([mods] * nsub), x)


ROUTE_LANE_BLOCK = 256
TOK_RADIX = 256


def _excl_cumsum_lanes(mask_f, upper_bf):
    n = mask_f.shape[1]
    b = upper_bf.shape[0]
    outs = []
    carry = jnp.zeros((mask_f.shape[0], 1), F32)
    for j in range(n // b):
        blk = mask_f[:, j * b:(j + 1) * b]
        outs.append(_dot(blk.astype(BF16), upper_bf) + carry)
        carry = carry + jnp.sum(blk, axis=1, keepdims=True)
    return jnp.concatenate(outs, axis=1)


def _route_kernel(lg_ref, o_ref, pos_scr, w_scr, *, N, CAP, E, RB, UNROLL):
    lg = lg_ref[...]
    lane = lax.broadcasted_iota(jnp.int32, lg.shape, 1)
    x = jnp.where(lane < E, lg, -jnp.inf)
    ex = jnp.exp(x - jnp.max(x, axis=1, keepdims=True))
    aff = ex / jnp.sum(ex, axis=1, keepdims=True)
    bits = pltpu.bitcast(jnp.concatenate([aff[q * N:(q + 1) * N, :].T[0:E, :] for q in range(RB)], axis=0), jnp.int32)

    def refine(i, prefix):
        cand = prefix | jnp.left_shift(jnp.int32(1), 30 - i)
        cnt = jnp.sum((bits >= cand).astype(F32), axis=1, keepdims=True)
        return jnp.where(cnt >= CAP, cand, prefix)

    thr = lax.fori_loop(0, 31, refine, jnp.zeros((RB * E, 1), jnp.int32))
    b = ROUTE_LANE_BLOCK
    r_i = lax.broadcasted_iota(jnp.int32, (b, b), 0)
    c_i = lax.broadcasted_iota(jnp.int32, (b, b), 1)
    upper = (r_i < c_i).astype(BF16)
    gt = bits > thr
    eq = bits == thr
    need = CAP - jnp.sum(gt.astype(F32), axis=1, keepdims=True)
    sel = gt | (eq & (_excl_cumsum_lanes(eq.astype(F32), upper) < need))
    pos = _excl_cumsum_lanes(sel.astype(F32), upper)
    pos_scr[...] = jnp.where(sel, pos, -1.0)

    hi, mid, lo = _split3(aff)
    tok = lax.broadcasted_iota(jnp.int32, lg.shape, 0) % N
    w = (hi.astype(F32) + pltpu.roll(mid.astype(F32), E, 1) + pltpu.roll(lo.astype(F32), 2 * E, 1)
         + jnp.where(lane == 3 * E, (tok // TOK_RADIX).astype(F32), 0.0)
         + jnp.where(lane == 3 * E + 1, (tok % TOK_RADIX).astype(F32), 0.0))
    w_scr[...] = w.astype(BF16)

    bn = min(N, 2 * ROUTE_LANE_BLOCK)
    p_iota = lax.broadcasted_iota(jnp.int32, (CAP, bn), 0).astype(F32)
    out_lane = lax.broadcasted_iota(jnp.int32, (CAP, LANES), 1)

    def per_row(i, carry):
        q = i // E
        e = i % E
        prow = pos_scr[pl.ds(i, 1), :]
        acc = jnp.zeros((CAP, LANES), F32)
        for j in range(N // bn):
            onehot = (prow[:, j * bn:(j + 1) * bn] == p_iota).astype(BF16)
            acc = acc + _dot(onehot, w_scr[pl.ds(pl.multiple_of(q * N + j * bn, bn), bn), :])
        gmask = (out_lane == e) | (out_lane == E + e) | (out_lane == 2 * E + e)
        gate = jnp.sum(jnp.where(gmask, acc, 0.0), axis=1, keepdims=True)
        tokid = acc[:, 3 * E:3 * E + 1] * float(TOK_RADIX) + acc[:, 3 * E + 1:3 * E + 2]
        o_ref[q, e] = jnp.where(out_lane == 0, tokid, jnp.where(out_lane == 1, gate, 0.0))
        return carry

    lax.fori_loop(0, RB * E, per_row, 0, unroll=UNROLL)


def _route_requests(logits, n_req, n_tok, cap):
    e = N_EXPERTS
    assert n_tok % ROUTE_LANE_BLOCK == 0 and n_tok <= TOK_RADIX * TOK_RADIX
    rb = max(1, min(n_req, 2048 // n_tok))
    assert n_req % rb == 0
    return pl.pallas_call(
        functools.partial(_route_kernel, N=n_tok, CAP=cap, E=e, RB=rb, UNROLL=4 if rb > 1 else 1),
        out_shape=jax.ShapeDtypeStruct((n_req, e, cap, LANES), F32),
        grid=(n_req // rb,),
        in_specs=[pl.BlockSpec((rb * n_tok, LANES), lambda r: (r, 0))],
        out_specs=pl.BlockSpec((rb, e, cap, LANES), lambda r: (r, 0, 0, 0)),
        scratch_shapes=[pltpu.VMEM((rb * e, n_tok), F32), pltpu.VMEM((rb * n_tok, LANES), BF16)],
        compiler_params=_cparams(("parallel",)),
        name="route",
    )(logits)


def _route(logits, dims):
    e = N_EXPERTS
    bp, bs, seq, dec_seq, p = dims["Bp"], dims["Bs"], dims["seq"], dims["dec_seq"], dims["P"]
    cap_p = CAPACITY_FACTOR * seq // e
    cap_s = CAPACITY_FACTOR * dec_seq // e
    rp = _route_requests(logits[:p], bp, seq, cap_p)
    rs = _route_requests(logits[p:], bs, dec_seq, cap_s)
    ip = rp[..., 0].astype(jnp.int32) + (jnp.arange(bp, dtype=jnp.int32) * seq)[:, None, None]
    is_ = rs[..., 0].astype(jnp.int32) + (p + jnp.arange(bs, dtype=jnp.int32) * dec_seq)[:, None, None]
    idx = jnp.concatenate([jnp.swapaxes(ip, 0, 1).reshape(e, bp * cap_p),
                           jnp.swapaxes(is_, 0, 1).reshape(e, bs * cap_s)], axis=1)
    gate = jnp.concatenate([jnp.swapaxes(rp[..., 1], 0, 1).reshape(e, bp * cap_p),
                            jnp.swapaxes(rs[..., 1], 0, 1).reshape(e, bs * cap_s)], axis=1)
    return idx, gate


def _moe(x, norm_g, mods, layer, w_router_pad, w_gate, w_up, w_down, dims):
    e = N_EXPERTS
    h2, logits = _h2_router(x, norm_g, mods, layer, w_router_pad, dims)
    idx, gate = _route(logits, dims)
    return _moe_fused(x, h2, idx.reshape(-1), gate.reshape(-1, 1), w_gate, w_up, w_down, mods, layer, dims)


def _rope_tables(dims, dh):
    n, p = dims["dec_seq"], dims["P"]
    axis_dim = dh // 2
    tpos = jnp.arange(n)
    rowp = (tpos // GRID_W).astype(F32)
    colp = (tpos % GRID_W).astype(F32)
    inv = ROPE_BASE ** (-jnp.arange(0, axis_dim, 2, dtype=F32) / axis_dim)
    ar = rowp[:, None] * inv
    ac = colp[:, None] * inv
    cos = jnp.concatenate([jnp.cos(ar), jnp.cos(ar), jnp.cos(ac), jnp.cos(ac)], axis=1)
    sin = jnp.concatenate([-jnp.sin(ar), jnp.sin(ar), -jnp.sin(ac), jnp.sin(ac)], axis=1)
    cos = jnp.concatenate([jnp.ones((p, dh), F32), jnp.tile(cos, (dims["Bs"], 1))], axis=0)
    sin = jnp.concatenate([jnp.zeros((p, dh), F32), jnp.tile(sin, (dims["Bs"], 1))], axis=0)
    return cos, sin


def kernel(x_prompt, x_sample, state_mlstm_C, state_mlstm_n, state_mlstm_m, cache_attn_k, cache_attn_v,
           c, c_ctx, w_mod, b_mod, norm1_g, norm2_g, m_w_in, m_b_gate, m_norm_g, m_w_out,
           a_w_in, a_q_norm_g, a_k_norm_g, a_sink, a_w_out, e_w_router, e_w_gate, e_w_up, e_w_down):
    bp, seq, d = x_prompt.shape
    bs, dec_seq, _ = x_sample.shape
    depth = w_mod.shape[0]
    dims = dict(Bp=bp, Bs=bs, seq=seq, dec_seq=dec_seq, P=bp * seq, S=bs * dec_seq, D=d)
    nh = MLSTM_HEADS
    dv = d // nh
    dk = dv // 2
    qk, vw = nh * dk, nh * dv
    dh = d // ATTN_HEADS

    x = jnp.concatenate([x_prompt.reshape(bp * seq, d), x_sample.reshape(bs * dec_seq, d)], axis=0)

    cvecs = jnp.concatenate([c_ctx[None, :], c, jnp.zeros((N_GROUPS_PAD - 1 - bs, d), F32)], axis=0)
    mods = _adaln(cvecs, w_mod, b_mod).reshape(depth, N_GROUPS_PAD, N_MOD, 1, d)
    cos_t, sin_t = _rope_tables(dims, dh)

    new_c, new_n, new_m, new_k, new_v = [], [], [], [], []
    for l in range(depth):
        j = l // 2
        if l % 2 == 0:
            w_in = m_w_in[j]
            w_main = w_in[:, :2 * qk + 2 * vw].astype(BF16)
            w_gate = jnp.pad(w_in[:, 2 * qk + 2 * vw:], ((0, 0), (0, LANES - 4 * nh))).astype(BF16)
            b_gate = jnp.pad(m_b_gate[j], (0, LANES - 4 * nh)).reshape(1, LANES)
            main, gates = _mlstm_inproj(x, norm1_g[l].reshape(1, d), mods, l, w_main, w_gate, b_gate, dims)
            c0 = jnp.concatenate([jnp.zeros((bp, 2, nh, dk, dv), F32), state_mlstm_C[:, j]], axis=0)
            n0 = jnp.concatenate([jnp.zeros((bp, 2 * nh, dk), F32),
                                  state_mlstm_n[:, j].reshape(bs, 2 * nh, dk)], axis=0)
            m0 = jnp.concatenate([jnp.zeros((bp, 2 * nh), F32), state_mlstm_m[:, j].reshape(bs, 2 * nh)], axis=0)
            m0 = jnp.broadcast_to(m0[:, :, None], (bp + bs, 2 * nh, LANES))
            hf, hb, cfin, nfin, mfin = _mlstm_scan(main, gates, c0, n0, m0, dims)
            new_c.append(cfin[:bp])
            new_n.append(nfin[:bp].reshape(bp, 2, nh, dk))
            new_m.append(mfin[:bp, :, 0].reshape(bp, 2, nh))
            x = _mlstm_out(x, hf, hb, main, m_norm_g[j].reshape(1, vw), m_w_out[j].astype(BF16), mods, l, dims)
        else:
            qh, kvh = _attn_inproj(x, norm1_g[l].reshape(1, d), mods, l, a_w_in[j].astype(BF16),
                                   a_q_norm_g[j].reshape(1, dh), a_k_norm_g[j].reshape(1, dh), cos_t, sin_t, dims)
            kvw = ATTN_KV_HEADS * dh
            new_k.append(kvh[:bp * seq, :kvw].reshape(bp, seq, ATTN_KV_HEADS, dh).transpose(0, 2, 1, 3))
            new_v.append(kvh[:bp * seq, kvw:].reshape(bp, seq, ATTN_KV_HEADS, dh).transpose(0, 2, 1, 3))
            o = _attention(qh, kvh, cache_attn_k[:, j], cache_attn_v[:, j], a_sink[j], dims)
            x = _proj_res(x, o, a_w_out[j].astype(BF16), mods, l, 2, dims)
        w_router_pad = jnp.pad(e_w_router[l], ((0, 0), (0, LANES - N_EXPERTS))).astype(BF16)
        x = _moe(x, norm2_g[l].reshape(1, d), mods, l, w_router_pad, e_w_gate, e_w_up, e_w_down, dims)

    p = bp * seq
    return (x[:p].reshape(bp, seq, d), x[p:].reshape(bs, dec_seq, d),
            jnp.stack(new_c, axis=1), jnp.stack(new_n, axis=1), jnp.stack(new_m, axis=1),
            jnp.stack(new_k, axis=1), jnp.stack(new_v, axis=1))
```

```python
import functools

import jax
import jax.numpy as jnp
import numpy as np
from jax import lax
from jax.experimental import pallas as pl
from jax.experimental.pallas import tpu as pltpu

F32 = jnp.float32
BF16 = jnp.bfloat16

GRID_W = 64
MLSTM_HEADS = 8
ATTN_HEADS = 16
ATTN_KV_HEADS = 4
ATTN_GROUP = ATTN_HEADS // ATTN_KV_HEADS
WINDOW = 128
ROPE_BASE = 10000.0
N_EXPERTS = 16
CAPACITY_FACTOR = 2
N_MOD = 6
EPS = 1e-6
N_GROUPS_PAD = 8
LANES = 128
SUBLANES = 8
MLSTM_CHUNK = 256
ATTN_BLOCK = 128
VMEM_LIMIT = 56 * 1024 * 1024


def _cparams(sem):
    return pltpu.CompilerParams(dimension_semantics=sem, vmem_limit_bytes=VMEM_LIMIT)


def _dot(a, b):
    return jnp.dot(a, b, preferred_element_type=F32)


def _dot_nt(a, b):
    return lax.dot_general(a, b, (((1,), (1,)), ((), ())), preferred_element_type=F32)


def _group_of_block(i, rows_per_block, n_prompt, dec_seq):
    row = i * rows_per_block
    return jnp.where(row < n_prompt, 0, 1 + (row - n_prompt) // dec_seq)


def _norm_mod(x, g, shift, scale):
    y = x * lax.rsqrt(jnp.mean(x * x, axis=-1, keepdims=True) + EPS)
    return (y * g) * (1.0 + scale) + shift


def _log_sigmoid(x):
    return jnp.minimum(x, 0.0) - jnp.log(1.0 + jnp.exp(-jnp.abs(x)))


def _split3(x):
    hi = x.astype(BF16)
    r1 = x - hi.astype(F32)
    mid = r1.astype(BF16)
    lo = (r1 - mid.astype(F32)).astype(BF16)
    return hi, mid, lo


def _adaln_kernel(c_ref, w_ref, b_ref, o_ref):
    c = c_ref[...]
    s = (c * jax.nn.sigmoid(c)).astype(BF16)
    o_ref[...] = _dot(s, w_ref[...].astype(BF16)) + b_ref[...]


def _adaln(cvecs, w_mod, b_mod):
    depth, d, n = w_mod.shape
    tn = 1024
    return pl.pallas_call(
        _adaln_kernel,
        out_shape=jax.ShapeDtypeStruct((depth, N_GROUPS_PAD, n), F32),
        grid=(depth, n // tn),
        in_specs=[
            pl.BlockSpec((N_GROUPS_PAD, d), lambda l, j: (0, 0)),
            pl.BlockSpec((None, d, tn), lambda l, j: (l, 0, j)),
            pl.BlockSpec((None, 1, tn), lambda l, j: (l, 0, j)),
        ],
        out_specs=pl.BlockSpec((None, N_GROUPS_PAD, tn), lambda l, j: (l, 0, j)),
        compiler_params=_cparams(("parallel", "parallel")),
        name="adaln",
    )(cvecs, w_mod, b_mod.reshape(depth, 1, n))


def _mod_spec(layer, which, tm, n_prompt, dec_seq, d, ncols=None, with_j=False):
    if ncols is None:
        ncols = d
    if with_j:
        return pl.BlockSpec((None, None, None, 1, ncols),
                            lambda i, j: (layer, _group_of_block(i, tm, n_prompt, dec_seq), which, 0, 0))
    return pl.BlockSpec((None, None, None, 1, ncols),
                        lambda i: (layer, _group_of_block(i, tm, n_prompt, dec_seq), which, 0, 0))


def _mlstm_inproj_kernel(x_ref, g_ref, sh_ref, sc_ref, w_ref, wg_ref, bg_ref, o_ref, og_ref, h_scr,
                         *, nq_blocks, qscale):
    j = pl.program_id(1)

    @pl.when(j == 0)
    def _():
        h = _norm_mod(x_ref[...], g_ref[...], sh_ref[...], sc_ref[...]).astype(BF16)
        h_scr[...] = h
        og_ref[...] = _dot(h, wg_ref[...]) + bg_ref[...]

    acc = _dot(h_scr[...], w_ref[...])
    scale = jnp.where(j < nq_blocks, qscale, 1.0).astype(F32)
    o_ref[...] = (acc * scale).astype(o_ref.dtype)


def _mlstm_inproj(x, norm_g, mods, layer, w_main, w_gate, b_gate, dims):
    t, d = x.shape
    n = w_main.shape[1]
    tm, tn = 1024, 512
    dk = d // MLSTM_HEADS // 2
    kern = functools.partial(_mlstm_inproj_kernel, nq_blocks=(MLSTM_HEADS * dk) // tn, qscale=float(dk) ** -0.5)
    return pl.pallas_call(
        kern,
        out_shape=(jax.ShapeDtypeStruct((t, n), BF16), jax.ShapeDtypeStruct((t, LANES), F32)),
        grid=(t // tm, n // tn),
        in_specs=[
            pl.BlockSpec((tm, d), lambda i, j: (i, 0)),
            pl.BlockSpec((1, d), lambda i, j: (0, 0)),
            _mod_spec(layer, 0, tm, dims["P"], dims["dec_seq"], d, with_j=True),
            _mod_spec(layer, 1, tm, dims["P"], dims["dec_seq"], d, with_j=True),
            pl.BlockSpec((d, tn), lambda i, j: (0, j)),
            pl.BlockSpec((d, LANES), lambda i, j: (0, 0)),
            pl.BlockSpec((1, LANES), lambda i, j: (0, 0)),
        ],
        out_specs=(pl.BlockSpec((tm, tn), lambda i, j: (i, j)),
                   pl.BlockSpec((tm, LANES), lambda i, j: (i, 0))),
        scratch_shapes=[pltpu.VMEM((tm, d), BF16)],
        compiler_params=_cparams(("parallel", "arbitrary")),
        name="mlstm_inproj",
    )(x, norm_g, mods, mods, w_main, w_gate, b_gate)


def _mlstm_kernel(fblk, bblk, seq, first, last,
                  qf, kf, vf, gf, qb, kb, vb, gb, c0, n0, m0,
                  hf, hb, cfin, nfin, mfin, c_scr, n_scr, m_scr, *, L, NH, DK, DV):
    s = pl.program_id(0)

    @pl.when(first[s] == 1)
    def _():
        c_scr[...] = c0[...]
        n_scr[...] = n0[...]
        m_scr[...] = m0[...]

    row = lax.broadcasted_iota(jnp.int32, (L, L), 0)
    col = lax.broadcasted_iota(jnp.int32, (L, L), 1)
    tril = col <= row
    triu = col >= row
    tril_bf = tril.astype(BF16)
    triu_bf = triu.astype(BF16)

    dirs = ((qf, kf, vf, gf, hf, tril, tril_bf, triu_bf, L - 1),
            (qb, kb, vb, gb, hb, triu, triu_bf, tril_bf, 0))
    for d, (q_ref, k_ref, v_ref, g_ref, h_ref, keep, keep_bf, keep_t_bf, last_row) in enumerate(dirs):
        g = g_ref[...]
        lf = _log_sigmoid(g)
        l1, l2, l3 = _split3(lf)
        bcol = _dot(keep_bf, l1) + _dot(keep_bf, l2) + _dot(keep_bf, l3)
        g_t = g.T
        lf_t = _log_sigmoid(g_t)
        t1, t2, t3 = _split3(lf_t)
        brow = _dot(t1, keep_t_bf) + _dot(t2, keep_t_bf) + _dot(t3, keep_t_bf)
        base = 2 * NH * d
        for h in range(NH):
            ci = base + h
            cf = base + NH + h
            sr = NH * d + h
            li_c = g[:, ci:ci + 1]
            b_c = bcol[:, cf:cf + 1]
            li_r = g_t[ci:ci + 1, :]
            b_r = brow[cf:cf + 1, :]
            m_prev = m_scr[sr:sr + 1, 0:1]
            dmat = jnp.where(keep, (b_c - b_r) + li_r, -jnp.inf)
            gv = b_c + m_prev
            m_t = jnp.maximum(gv, jnp.max(dmat, axis=1, keepdims=True))
            w_inter = jnp.exp(gv - m_t)
            w_intra = jnp.exp(dmat - m_t)
            q = q_ref[:, h * DK:(h + 1) * DK]
            k = k_ref[:, h * DK:(h + 1) * DK]
            v = v_ref[:, h * DV:(h + 1) * DV]
            sc = _dot_nt(q, k) * w_intra
            c_st = c_scr[d, h]
            n_st = n_scr[sr:sr + 1, :]
            num = w_inter * _dot(q, c_st.astype(BF16)) + _dot(sc.astype(BF16), v)
            den = (w_inter * jnp.sum(q.astype(F32) * n_st, axis=1, keepdims=True)
                   + jnp.sum(sc, axis=1, keepdims=True))
            h_ref[:, h * DV:(h + 1) * DV] = num / jnp.maximum(jnp.abs(den), jnp.exp(-m_t))
            b_last = b_c[last_row:last_row + 1, :]
            a = (b_last - b_c) + li_c
            m_loc = jnp.max(a, axis=0, keepdims=True)
            w = jnp.exp(a - m_loc)
            kw = k.astype(F32) * w
            c_loc = _dot(kw.T.astype(BF16), v)
            n_loc = jnp.sum(kw, axis=0, keepdims=True)
            m_new = jnp.maximum(b_last + m_prev, m_loc)
            w_old = jnp.exp(b_last + m_prev - m_new)
            w_new = jnp.exp(m_loc - m_new)
            c_scr[d, h] = w_old * c_st + w_new * c_loc
            n_scr[sr:sr + 1, :] = w_old * n_st + w_new * n_loc
            m_scr[sr:sr + 1, :] = jnp.broadcast_to(m_new, (1, LANES))

    @pl.when(last[s] == 1)
    def _():
        cfin[...] = c_scr[...]
        nfin[...] = n_scr[...]
        mfin[...] = m_scr[...]


def _mlstm_scan(main, gates, c0, n0, m0, dims):
    t = main.shape[0]
    d_model = dims["D"]
    nh = MLSTM_HEADS
    dv = d_model // nh
    dk = dv // 2
    L = MLSTM_CHUNK
    assert dims["seq"] == L and dims["dec_seq"] % L == 0
    bp, bs = dims["Bp"], dims["Bs"]
    ncs = dims["dec_seq"] // L
    pblk = dims["P"] // L
    fblk = np.concatenate([np.arange(bp), pblk + np.arange(bs * ncs)]).astype(np.int32)
    bblk = np.concatenate([np.arange(bp),
                           pblk + (np.arange(bs)[:, None] * ncs + (ncs - 1 - np.arange(ncs))[None, :]).reshape(-1)]
                          ).astype(np.int32)
    seq = np.concatenate([np.arange(bp), bp + np.repeat(np.arange(bs), ncs)]).astype(np.int32)
    first = np.concatenate([np.ones(bp), (np.tile(np.arange(ncs), bs) == 0)]).astype(np.int32)
    last = np.concatenate([np.ones(bp), (np.tile(np.arange(ncs), bs) == ncs - 1)]).astype(np.int32)
    nseq = bp + bs
    nsteps = fblk.shape[0]
    qk = nh * dk
    vw = nh * dv
    kern = functools.partial(_mlstm_kernel, L=L, NH=nh, DK=dk, DV=dv)

    def fmap(cb):
        return lambda s, fb, bb, sq, fi, la: (fb[s], cb)

    def bmap(cb):
        return lambda s, fb, bb, sq, fi, la: (bb[s], cb)

    def smap(nd):
        return lambda s, fb, bb, sq, fi, la: (sq[s],) + (0,) * nd

    grid_spec = pltpu.PrefetchScalarGridSpec(
        num_scalar_prefetch=5,
        grid=(nsteps,),
        in_specs=[
            pl.BlockSpec((L, qk), fmap(0)), pl.BlockSpec((L, qk), fmap(1)),
            pl.BlockSpec((L, vw), fmap(2 * qk // vw)), pl.BlockSpec((L, LANES), fmap(0)),
            pl.BlockSpec((L, qk), bmap(0)), pl.BlockSpec((L, qk), bmap(1)),
            pl.BlockSpec((L, vw), bmap(2 * qk // vw)), pl.BlockSpec((L, LANES), bmap(0)),
            pl.BlockSpec((None, 2, nh, dk, dv), smap(4)),
            pl.BlockSpec((None, 2 * nh, dk), smap(2)),
            pl.BlockSpec((None, 2 * nh, LANES), smap(2)),
        ],
        out_specs=(
            pl.BlockSpec((L, vw), fmap(0)), pl.BlockSpec((L, vw), bmap(0)),
            pl.BlockSpec((None, 2, nh, dk, dv), smap(4)),
            pl.BlockSpec((None, 2 * nh, dk), smap(2)),
            pl.BlockSpec((None, 2 * nh, LANES), smap(2)),
        ),
        scratch_shapes=[pltpu.VMEM((2, nh, dk, dv), F32), pltpu.VMEM((2 * nh, dk), F32),
                        pltpu.VMEM((2 * nh, LANES), F32)],
    )
    return pl.pallas_call(
        kern,
        out_shape=(jax.ShapeDtypeStruct((t, vw), F32), jax.ShapeDtypeStruct((t, vw), F32),
                   jax.ShapeDtypeStruct((nseq, 2, nh, dk, dv), F32),
                   jax.ShapeDtypeStruct((nseq, 2 * nh, dk), F32),
                   jax.ShapeDtypeStruct((nseq, 2 * nh, LANES), F32)),
        grid_spec=grid_spec,
        compiler_params=_cparams(("arbitrary",)),
        name="mlstm_scan",
    )(jnp.asarray(fblk), jnp.asarray(bblk), jnp.asarray(seq), jnp.asarray(first), jnp.asarray(last),
      main, main, main, gates, main, main, main, gates, c0, n0, m0)


def _mlstm_out_kernel(x_ref, hf_ref, hb_ref, o_ref, ng_ref, w_ref, g1_ref, out_ref, *, NH, DV):
    parts = []
    for h in range(NH):
        sl = slice(h * DV, (h + 1) * DV)
        hs = hf_ref[:, sl] + hb_ref[:, sl]
        hs = hs * lax.rsqrt(jnp.mean(hs * hs, axis=-1, keepdims=True) + EPS)
        hs = hs * ng_ref[:, sl]
        parts.append((hs * jax.nn.sigmoid(o_ref[:, sl].astype(F32))).astype(BF16))
    z = jnp.concatenate(parts, axis=1)
    out_ref[...] = x_ref[...] + g1_ref[...] * _dot(z, w_ref[...])


def _mlstm_out(x, hf, hb, main, norm_g, w_out, mods, layer, dims):
    t, d = x.shape
    tm = 256
    vw = hf.shape[1]
    kern = functools.partial(_mlstm_out_kernel, NH=MLSTM_HEADS, DV=vw // MLSTM_HEADS)
    ocol = (main.shape[1] - vw) // vw
    return pl.pallas_call(
        kern,
        out_shape=jax.ShapeDtypeStruct((t, d), F32),
        grid=(t // tm,),
        in_specs=[
            pl.BlockSpec((tm, d), lambda i: (i, 0)),
            pl.BlockSpec((tm, vw), lambda i: (i, 0)),
            pl.BlockSpec((tm, vw), lambda i: (i, 0)),
            pl.BlockSpec((tm, vw), lambda i: (i, ocol)),
            pl.BlockSpec((1, vw), lambda i: (0, 0)),
            pl.BlockSpec((vw, d), lambda i: (0, 0)),
            _mod_spec(layer, 2, tm, dims["P"], dims["dec_seq"], d),
        ],
        out_specs=pl.BlockSpec((tm, d), lambda i: (i, 0)),
        compiler_params=_cparams(("parallel",)),
        name="mlstm_out",
    )(x, hf, hb, main, norm_g, w_out, mods)


def _attn_inproj_kernel(x_ref, g_ref, sh_ref, sc_ref, w_ref, qn_ref, kn_ref, cos_ref, sin_ref,
                        oq_ref, okv_ref, h_scr, *, nq_blocks, heads_per_block, DH):
    j = pl.program_id(1)

    @pl.when(j == 0)
    def _():
        h_scr[...] = _norm_mod(x_ref[...], g_ref[...], sh_ref[...], sc_ref[...]).astype(BF16)

    acc = _dot(h_scr[...], w_ref[...])

    def normed(gain):
        cos = cos_ref[...]
        sin = sin_ref[...]
        lane = lax.broadcasted_iota(jnp.int32, cos.shape, 1)
        first_half = (lane % (DH // 2)) < (DH // 4)
        outs = []
        for hh in range(heads_per_block):
            a = acc[:, hh * DH:(hh + 1) * DH]
            r = a * lax.rsqrt(jnp.mean(a * a, axis=-1, keepdims=True) + EPS) * gain
            partner = jnp.where(first_half, pltpu.roll(r, DH - DH // 4, 1), pltpu.roll(r, DH // 4, 1))
            outs.append(r * cos + partner * sin)
        return jnp.concatenate(outs, axis=1)

    @pl.when(j < nq_blocks)
    def _():
        oq_ref[...] = normed(qn_ref[...]).astype(oq_ref.dtype)

    @pl.when(j == nq_blocks)
    def _():
        okv_ref[...] = normed(kn_ref[...])

    @pl.when(j > nq_blocks)
    def _():
        okv_ref[...] = acc


def _attn_inproj(x, norm_g, mods, layer, w_in, qn_g, kn_g, cos_t, sin_t, dims):
    t, d = x.shape
    n = w_in.shape[1]
    dh = d // ATTN_HEADS
    tm, tn = 1024, 512
    assert ATTN_KV_HEADS * dh == tn
    nqb = (ATTN_HEADS * dh) // tn
    kern = functools.partial(_attn_inproj_kernel, nq_blocks=nqb, heads_per_block=tn // dh, DH=dh)
    return pl.pallas_call(
        kern,
        out_shape=(jax.ShapeDtypeStruct((t, ATTN_HEADS * dh), BF16),
                   jax.ShapeDtypeStruct((t, 2 * ATTN_KV_HEADS * dh), F32)),
        grid=(t // tm, n // tn),
        in_specs=[
            pl.BlockSpec((tm, d), lambda i, j: (i, 0)),
            pl.BlockSpec((1, d), lambda i, j: (0, 0)),
            _mod_spec(layer, 0, tm, dims["P"], dims["dec_seq"], d, with_j=True),
            _mod_spec(layer, 1, tm, dims["P"], dims["dec_seq"], d, with_j=True),
            pl.BlockSpec((d, tn), lambda i, j: (0, j)),
            pl.BlockSpec((1, dh), lambda i, j: (0, 0)),
            pl.BlockSpec((1, dh), lambda i, j: (0, 0)),
            pl.BlockSpec((tm, dh), lambda i, j: (i, 0)),
            pl.BlockSpec((tm, dh), lambda i, j: (i, 0)),
        ],
        out_specs=(pl.BlockSpec((tm, tn), lambda i, j: (i, jnp.minimum(j, nqb - 1))),
                   pl.BlockSpec((tm, tn), lambda i, j: (i, jnp.maximum(j - nqb, 0)))),
        scratch_shapes=[pltpu.VMEM((tm, d), BF16)],
        compiler_params=_cparams(("parallel", "arbitrary")),
        name="attn_inproj",
    )(x, norm_g, mods, mods, w_in, qn_g, kn_g, cos_t, sin_t)


def _softmax_sink_rows(s_list, sink_col):
    m = sink_col
    for s in s_list:
        m = jnp.maximum(m, jnp.max(s, axis=1, keepdims=True))
    ps = [jnp.exp(s - m) for s in s_list]
    tot = jnp.exp(sink_col - m)
    for p in ps:
        tot = tot + jnp.sum(p, axis=1, keepdims=True)
    return ps, tot


def _ctx_attn_kernel(sink_ref, q_ref, k_ref, v_ref, o_ref, *, G, DH):
    kv = pl.program_id(1)
    k = k_ref[...].astype(BF16)
    v = v_ref[...].astype(BF16)
    scale = float(DH) ** -0.5
    for g in range(G):
        q = q_ref[:, g * DH:(g + 1) * DH]
        s = _dot_nt(q, k) * scale
        sink = jnp.full((s.shape[0], 1), sink_ref[kv * G + g], F32)
        (p,), tot = _softmax_sink_rows([s], sink)
        o_ref[:, g * DH:(g + 1) * DH] = (_dot(p.astype(BF16), v) / tot).astype(o_ref.dtype)


def _lat_attn_kernel(sink_ref, q_ref, kp_ref, kc_ref, kn_ref, vp_ref, vc_ref, vn_ref, kx_ref, vx_ref, o_ref,
                     *, G, DH, NB, BLK):
    kv = pl.program_id(1)
    i = pl.program_id(2)
    scale = float(DH) ** -0.5
    q = jnp.concatenate([q_ref[:, g * DH:(g + 1) * DH] for g in range(G)], axis=0)
    rows = G * BLK
    r = lax.broadcasted_iota(jnp.int32, (rows, BLK), 0) % BLK
    c = lax.broadcasted_iota(jnp.int32, (rows, BLK), 1)
    s_p = jnp.where((c >= r) & (i > 0), _dot_nt(q, kp_ref[...].astype(BF16)) * scale, -jnp.inf)
    s_m = _dot_nt(q, kc_ref[...].astype(BF16)) * scale
    s_n = jnp.where((c <= r) & (i < NB - 1), _dot_nt(q, kn_ref[...].astype(BF16)) * scale, -jnp.inf)
    s_x = _dot_nt(q, kx_ref[...].astype(BF16)) * scale
    rid = lax.broadcasted_iota(jnp.int32, (rows, 1), 0) // BLK
    sink = jnp.zeros((rows, 1), F32)
    for g in range(G):
        sink = jnp.where(rid == g, sink_ref[kv * G + g], sink)
    (p_p, p_m, p_n, p_x), tot = _softmax_sink_rows([s_p, s_m, s_n, s_x], sink)
    o = (_dot(p_p.astype(BF16), vp_ref[...].astype(BF16)) + _dot(p_m.astype(BF16), vc_ref[...].astype(BF16))
         + _dot(p_n.astype(BF16), vn_ref[...].astype(BF16)) + _dot(p_x.astype(BF16), vx_ref[...].astype(BF16))) / tot
    for g in range(G):
        o_ref[:, g * DH:(g + 1) * DH] = o[g * BLK:(g + 1) * BLK, :].astype(o_ref.dtype)


def _attention(qh, kvh, cache_k, cache_v, sink, dims):
    t = qh.shape[0]
    dh = qh.shape[1] // ATTN_HEADS
    G, KV = ATTN_GROUP, ATTN_KV_HEADS
    bp, bs, seq, dec_seq = dims["Bp"], dims["Bs"], dims["seq"], dims["dec_seq"]
    gw = G * dh
    o_ctx = pl.pallas_call(
        functools.partial(_ctx_attn_kernel, G=G, DH=dh),
        out_shape=jax.ShapeDtypeStruct((dims["P"], ATTN_HEADS * dh), BF16),
        grid_spec=pltpu.PrefetchScalarGridSpec(
            num_scalar_prefetch=1,
            grid=(bp, KV),
            in_specs=[
                pl.BlockSpec((seq, gw), lambda b, kv, sk: (b, kv)),
                pl.BlockSpec((seq, dh), lambda b, kv, sk: (b, kv)),
                pl.BlockSpec((seq, dh), lambda b, kv, sk: (b, KV + kv)),
            ],
            out_specs=pl.BlockSpec((seq, gw), lambda b, kv, sk: (b, kv)),
        ),
        compiler_params=_cparams(("parallel", "parallel")),
        name="ctx_attn",
    )(sink, qh, kvh, kvh)
    blk = ATTN_BLOCK
    assert blk == WINDOW and dec_seq % blk == 0 and dims["P"] % blk == 0
    nb = dec_seq // blk
    off = dims["P"] // blk

    def qmap(r, kv, i, sk):
        return (off + r * nb + i, kv)

    def omap(r, kv, i, sk):
        return (r * nb + i, kv)

    def kmap(delta, colbase):
        def f(r, kv, i, sk):
            return (off + r * nb + jnp.clip(i + delta, 0, nb - 1), colbase + kv)
        return f

    o_lat = pl.pallas_call(
        functools.partial(_lat_attn_kernel, G=G, DH=dh, NB=nb, BLK=blk),
        out_shape=jax.ShapeDtypeStruct((dims["S"], ATTN_HEADS * dh), BF16),
        grid_spec=pltpu.PrefetchScalarGridSpec(
            num_scalar_prefetch=1,
            grid=(bs, KV, nb),
            in_specs=[
                pl.BlockSpec((blk, gw), qmap),
                pl.BlockSpec((blk, dh), kmap(-1, 0)), pl.BlockSpec((blk, dh), kmap(0, 0)),
                pl.BlockSpec((blk, dh), kmap(1, 0)),
                pl.BlockSpec((blk, dh), kmap(-1, KV)), pl.BlockSpec((blk, dh), kmap(0, KV)),
                pl.BlockSpec((blk, dh), kmap(1, KV)),
                pl.BlockSpec((None, None, cache_k.shape[2], dh), lambda r, kv, i, sk: (r, kv, 0, 0)),
                pl.BlockSpec((None, None, cache_v.shape[2], dh), lambda r, kv, i, sk: (r, kv, 0, 0)),
            ],
            out_specs=pl.BlockSpec((blk, gw), omap),
        ),
        compiler_params=_cparams(("parallel", "parallel", "arbitrary")),
        name="lat_attn",
    )(sink, qh, kvh, kvh, kvh, kvh, kvh, kvh, cache_k, cache_v)
    return jnp.concatenate([o_ctx, o_lat], axis=0)


def _proj_res_kernel(x_ref, a_ref, w_ref, g1_ref, out_ref):
    out_ref[...] = x_ref[...] + g1_ref[...] * _dot(a_ref[...], w_ref[...])


def _proj_res(x, a, w, mods, layer, which, dims):
    t, d = x.shape
    tm = 512
    return pl.pallas_call(
        _proj_res_kernel,
        out_shape=jax.ShapeDtypeStruct((t, d), F32),
        grid=(t // tm,),
        in_specs=[
            pl.BlockSpec((tm, d), lambda i: (i, 0)),
            pl.BlockSpec((tm, a.shape[1]), lambda i: (i, 0)),
            pl.BlockSpec(w.shape, lambda i: (0, 0)),
            _mod_spec(layer, which, tm, dims["P"], dims["dec_seq"], d),
        ],
        out_specs=pl.BlockSpec((tm, d), lambda i: (i, 0)),
        compiler_params=_cparams(("parallel",)),
        name="proj_res",
    )(x, a, w, mods)


def _h2_router_kernel(x_ref, g_ref, sh_ref, sc_ref, wr_ref, h_ref, lg_ref):
    hb = _norm_mod(x_ref[...], g_ref[...], sh_ref[...], sc_ref[...]).astype(BF16)
    lg_ref[...] = _dot(hb, wr_ref[...])
    bits = pltpu.bitcast(hb.astype(F32), jnp.int32)
    half = bits.shape[1] // 2
    lo = bits[:, :half]
    h_ref[...] = (bits[:, half:] & jnp.int32(-65536)) | lax.shift_right_logical(lo, jnp.full(lo.shape, 16, jnp.int32))


def _h2_router(x, norm_g, mods, layer, w_router_pad, dims):
    t, d = x.shape
    tm = 512
    return pl.pallas_call(
        _h2_router_kernel,
        out_shape=(jax.ShapeDtypeStruct((t, d // 2), jnp.int32), jax.ShapeDtypeStruct((t, LANES), F32)),
        grid=(t // tm,),
        in_specs=[
            pl.BlockSpec((tm, d), lambda i: (i, 0)),
            pl.BlockSpec((1, d), lambda i: (0, 0)),
            _mod_spec(layer, 3, tm, dims["P"], dims["dec_seq"], d),
            _mod_spec(layer, 4, tm, dims["P"], dims["dec_seq"], d),
            pl.BlockSpec((d, LANES), lambda i: (0, 0)),
        ],
        out_specs=(pl.BlockSpec((tm, d // 2), lambda i: (i, 0)), pl.BlockSpec((tm, LANES), lambda i: (i, 0))),
        compiler_params=_cparams(("parallel",)),
        name="h2_router",
    )(x, norm_g, mods, mods, w_router_pad)


def _moe_kernel(idx_ref, h2_hbm, wg_ref, wu_ref, wd_ref, gate_ref, *rest, TM, TF, NF, MT, NTILES, NSUB):
    g2_refs = rest[:NSUB]
    x_hbm, o_hbm, xf_ref, xb_ref, hm_ref, acc_ref, sem_x, sem_acc, sem_out = rest[NSUB:]
    del x_hbm
    s = pl.program_id(2)
    n = pl.program_id(0) * MT + pl.program_id(1)
    base = n * TM
    nh = NF // 2
    chunk_h2 = TM // NF
    chunk_rw = TM // nh
    sub = TM // NSUB
    dhalf = xb_ref.shape[1] // 2
    RH = dhalf // LANES
    RX = 2 * RH
    PITCH = acc_ref.shape[0] // TM

    def aligned(start, align):
        return start if isinstance(start, int) else pl.multiple_of(start, align)

    def tok_rows(tok, rows_per_tok):
        return pl.ds(aligned(tok * rows_per_tok, rows_per_tok), rows_per_tok)

    def h2_wait():
        pltpu.make_async_copy(h2_hbm.at[pl.ds(0, TM * RH), :], xf_ref, sem_x).wait()

    def stream_wait(sem):
        pltpu.make_async_copy(o_hbm.at[pl.ds(0, TM * RX), :], acc_ref.at[pl.ds(0, TM * RX), :], sem).wait()

    def h2_copy(tile_base, r):
        tok = idx_ref[tile_base + r]
        return pltpu.make_async_copy(h2_hbm.at[tok_rows(tok, RH), :], xf_ref.at[tok_rows(r, RH), :], sem_x)

    def acc_rows(r):
        return pl.ds(aligned(r * PITCH, SUBLANES), RX)

    def out_copy(tile_base, r):
        tok = idx_ref[tile_base + r]
        return pltpu.make_async_copy(acc_ref.at[acc_rows(r), :], o_hbm.at[tok_rows(tok, RX), :], sem_out)

    def issue_h2(tile_base, r0, cnt):
        for r in range(cnt):
            h2_copy(tile_base, r0 + r).start()

    def issue_acc(r0, cnt):
        for r in range(cnt):
            tok = idx_ref[base + r0 + r]
            pltpu.make_async_copy(o_hbm.at[tok_rows(tok, RX), :], acc_ref.at[acc_rows(r0 + r), :], sem_acc).start()

    def issue_out(tile_base, r0, cnt):
        for r in range(cnt):
            out_copy(tile_base, r0 + r).start()

    @pl.when((s == 0) & (n == 0))
    def _():
        def body(r, carry):
            h2_copy(0, r).start()
            return carry
        lax.fori_loop(0, TM, body, 0)

    @pl.when(s == 0)
    def _():
        h2_wait()
        for j in range(RH):
            w = xf_ref[pl.ds(j, TM, stride=RH), :]
            cl = slice(j * LANES, (j + 1) * LANES)
            ch = slice(dhalf + j * LANES, dhalf + (j + 1) * LANES)
            xb_ref[:, cl] = pltpu.bitcast(w << 16, F32).astype(BF16)
            xb_ref[:, ch] = pltpu.bitcast(w & jnp.int32(-65536), F32).astype(BF16)

    def hidden_block(issue, r0, cnt):
        wg = wg_ref[...].astype(BF16)
        wu = wu_ref[...].astype(BF16)
        part = cnt // NSUB
        for k in range(NSUB):
            rs = slice(k * sub, (k + 1) * sub)
            if issue is not None:
                issue(r0 + k * part, part)
            xb = xb_ref[rs, :]
            a = _dot(xb, wg)
            u = _dot(xb, wu)
            hm_ref[s, rs, :] = (a * jax.nn.sigmoid(a) * u).astype(BF16)

    @pl.when((s < nh) & (n == 0))
    def _():
        hidden_block(None, 0, 0)

    @pl.when((s < nh) & (n > 0))
    def _():
        hidden_block(functools.partial(issue_out, base - TM), s * chunk_rw, chunk_rw)

    @pl.when((s == nh) & (n > 0))
    def _():
        stream_wait(sem_out)

    @pl.when((s >= nh) & (s < NF))
    def _():
        hidden_block(issue_acc, (s - nh) * chunk_rw, chunk_rw)

    @pl.when(s == NF)
    def _():
        stream_wait(sem_acc)

    for c in range(NF):
        @pl.when(s == NF + c)
        def _(c=c):
            cols = slice(c * TF, (c + 1) * TF)
            wd = [wd_ref[f * TF:(f + 1) * TF, :].astype(BF16) for f in range(NF)]
            part = chunk_h2 // NSUB
            for k in range(NSUB):
                rs = slice(k * sub, (k + 1) * sub)
                issue_h2(base + TM, c * chunk_h2 + k * part, part)
                y = _dot(hm_ref[0, rs, :], wd[0])
                for f in range(1, NF):
                    y = y + _dot(hm_ref[f, rs, :], wd[f])
                y = (y * gate_ref[rs, :]) * g2_refs[k][:, cols]
                for jj in range(TF // LANES):
                    rsel = pl.ds(k * sub * PITCH + c * (TF // LANES) + jj, sub, stride=PITCH)
                    acc_ref[rsel, :] = acc_ref[rsel, :] + y[:, jj * LANES:(jj + 1) * LANES]

    @pl.when((s == 2 * NF - 1) & (n == NTILES - 1))
    def _():
        def body(r, carry):
            out_copy(base, r).start()
            return carry
        lax.fori_loop(0, TM, body, 0)
        stream_wait(sem_out)
        h2_wait()


def _moe_fused(x, h2, idx_flat, gates_col, w_gate, w_up, w_down, mods, layer, dims):
    t, d = x.shape
    e = N_EXPERTS
    dff = w_gate.shape[-1]
    rows = idx_flat.shape[0] // e
    rows_p = dims["Bp"] * (CAPACITY_FACTOR * dims["seq"] // e)
    cap_s = CAPACITY_FACTOR * dims["dec_seq"] // e
    mt = 2
    tm = rows // mt
    sub = cap_s
    nsub = tm // sub
    assert rows % mt == 0 and tm % sub == 0 and rows_p % sub == 0 and rows_p <= tm
    tf = 256
    nf = dff // tf
    assert d == dff and nf % 2 == 0 and tm % (nf * nsub) == 0
    ntiles = e * mt
    idx_pad = jnp.concatenate([idx_flat, jnp.zeros((tm,), jnp.int32)])

    def g2_spec(k):
        def imap(ex, m, s, idx):
            r = m * tm + k * sub
            return (layer, jnp.where(r < rows_p, 0, 1 + (r - rows_p) // cap_s), 5, 0, 0)
        return pl.BlockSpec((None, None, None, 1, d), imap)

    rx = d // LANES
    rh = rx // 2
    assert rx % SUBLANES == 0 and rh % SUBLANES == 0
    acc_pitch = rx + SUBLANES
    x_tm = x.reshape(t * rx, LANES)
    h2 = h2.reshape(t * rh, LANES)
    kern = functools.partial(_moe_kernel, TM=tm, TF=tf, NF=nf, MT=mt, NTILES=ntiles, NSUB=nsub)
    out = pl.pallas_call(
        kern,
        out_shape=jax.ShapeDtypeStruct((t * rx, LANES), F32),
        grid_spec=pltpu.PrefetchScalarGridSpec(
            num_scalar_prefetch=1,
            grid=(e, mt, 2 * nf),
            in_specs=[
                pl.BlockSpec(memory_space=pl.ANY),
                pl.BlockSpec((None, None, d, tf), lambda ex, m, s, idx: (layer, ex, 0, jnp.minimum(s, nf - 1))),
                pl.BlockSpec((None, None, d, tf), lambda ex, m, s, idx: (layer, ex, 0, jnp.minimum(s, nf - 1))),
                pl.BlockSpec((None, None, dff, tf), lambda ex, m, s, idx: (layer, ex, 0, jnp.maximum(s - nf, 0))),
                pl.BlockSpec((tm, 1), lambda ex, m, s, idx: (ex * mt + m, 0)),
            ] + [g2_spec(k) for k in range(nsub)] + [pl.BlockSpec(memory_space=pl.ANY)],
            out_specs=pl.BlockSpec(memory_space=pl.ANY),
            scratch_shapes=[
                pltpu.VMEM((tm * rh, LANES), jnp.int32), pltpu.VMEM((tm, d), BF16), pltpu.VMEM((nf, tm, tf), BF16),
                pltpu.VMEM((tm * acc_pitch, LANES), F32),
                pltpu.SemaphoreType.DMA, pltpu.SemaphoreType.DMA, pltpu.SemaphoreType.DMA,
            ],
        ),
        input_output_aliases={6 + nsub: 0},
        compiler_params=_cparams(("arbitrary", "arbitrary", "arbitrary")),
        name="moe_fused",
    )(idx_pad, h2, w_gate, w_up, w_down, gates_col, *([mods] * nsub), x_tm)
    return out.reshape(t, d)


ROUTE_LANE_BLOCK = 256
TOK_RADIX = 256


def _excl_cumsum_lanes(mask_f, upper_bf):
    n = mask_f.shape[1]
    b = upper_bf.shape[0]
    outs = []
    carry = jnp.zeros((mask_f.shape[0], 1), F32)
    for j in range(n // b):
        blk = mask_f[:, j * b:(j + 1) * b]
        outs.append(_dot(blk.astype(BF16), upper_bf) + carry)
        carry = carry + jnp.sum(blk, axis=1, keepdims=True)
    return jnp.concatenate(outs, axis=1)


def _route_kernel(lg_ref, o_ref, pos_scr, w_scr, *, N, CAP, E, RB, UNROLL):
    lg = lg_ref[...]
    lane = lax.broadcasted_iota(jnp.int32, lg.shape, 1)
    x = jnp.where(lane < E, lg, -jnp.inf)
    ex = jnp.exp(x - jnp.max(x, axis=1, keepdims=True))
    aff = ex / jnp.sum(ex, axis=1, keepdims=True)
    bits = pltpu.bitcast(jnp.concatenate([aff[q * N:(q + 1) * N, :].T[0:E, :] for q in range(RB)], axis=0), jnp.int32)

    def refine(i, prefix):
        cand = prefix | jnp.left_shift(jnp.int32(1), 30 - i)
        cnt = jnp.sum((bits >= cand).astype(F32), axis=1, keepdims=True)
        return jnp.where(cnt >= CAP, cand, prefix)

    thr = lax.fori_loop(0, 31, refine, jnp.zeros((RB * E, 1), jnp.int32))
    b = ROUTE_LANE_BLOCK
    r_i = lax.broadcasted_iota(jnp.int32, (b, b), 0)
    c_i = lax.broadcasted_iota(jnp.int32, (b, b), 1)
    upper = (r_i < c_i).astype(BF16)
    gt = bits > thr
    eq = bits == thr
    need = CAP - jnp.sum(gt.astype(F32), axis=1, keepdims=True)
    sel = gt | (eq & (_excl_cumsum_lanes(eq.astype(F32), upper) < need))
    pos = _excl_cumsum_lanes(sel.astype(F32), upper)
    pos_scr[...] = jnp.where(sel, pos, -1.0)

    hi, mid, lo = _split3(aff)
    tok = lax.broadcasted_iota(jnp.int32, lg.shape, 0) % N
    w = (hi.astype(F32) + pltpu.roll(mid.astype(F32), E, 1) + pltpu.roll(lo.astype(F32), 2 * E, 1)
         + jnp.where(lane == 3 * E, (tok // TOK_RADIX).astype(F32), 0.0)
         + jnp.where(lane == 3 * E + 1, (tok % TOK_RADIX).astype(F32), 0.0))
    w_scr[...] = w.astype(BF16)

    bn = min(N, 2 * ROUTE_LANE_BLOCK)
    p_iota = lax.broadcasted_iota(jnp.int32, (CAP, bn), 0).astype(F32)
    out_lane = lax.broadcasted_iota(jnp.int32, (CAP, LANES), 1)

    def per_row(i, carry):
        q = i // E
        e = i % E
        prow = pos_scr[pl.ds(i, 1), :]
        acc = jnp.zeros((CAP, LANES), F32)
        for j in range(N // bn):
            onehot = (prow[:, j * bn:(j + 1) * bn] == p_iota).astype(BF16)
            acc = acc + _dot(onehot, w_scr[pl.ds(pl.multiple_of(q * N + j * bn, bn), bn), :])
        gmask = (out_lane == e) | (out_lane == E + e) | (out_lane == 2 * E + e)
        gate = jnp.sum(jnp.where(gmask, acc, 0.0), axis=1, keepdims=True)
        tokid = acc[:, 3 * E:3 * E + 1] * float(TOK_RADIX) + acc[:, 3 * E + 1:3 * E + 2]
        o_ref[q, e] = jnp.where(out_lane == 0, tokid, jnp.where(out_lane == 1, gate, 0.0))
        return carry

    lax.fori_loop(0, RB * E, per_row, 0, unroll=UNROLL)


def _route_requests(logits, n_req, n_tok, cap):
    e = N_EXPERTS
    assert n_tok % ROUTE_LANE_BLOCK == 0 and n_tok <= TOK_RADIX * TOK_RADIX
    rb = max(1, min(n_req, 2048 // n_tok))
    assert n_req % rb == 0
    return pl.pallas_call(
        functools.partial(_route_kernel, N=n_tok, CAP=cap, E=e, RB=rb, UNROLL=4 if rb > 1 else 1),
        out_shape=jax.ShapeDtypeStruct((n_req, e, cap, LANES), F32),
        grid=(n_req // rb,),
        in_specs=[pl.BlockSpec((rb * n_tok, LANES), lambda r: (r, 0))],
        out_specs=pl.BlockSpec((rb, e, cap, LANES), lambda r: (r, 0, 0, 0)),
        scratch_shapes=[pltpu.VMEM((rb * e, n_tok), F32), pltpu.VMEM((rb * n_tok, LANES), BF16)],
        compiler_params=_cparams(("parallel",)),
        name="route",
    )(logits)


def _route(logits, dims):
    e = N_EXPERTS
    bp, bs, seq, dec_seq, p = dims["Bp"], dims["Bs"], dims["seq"], dims["dec_seq"], dims["P"]
    cap_p = CAPACITY_FACTOR * seq // e
    cap_s = CAPACITY_FACTOR * dec_seq // e
    rp = _route_requests(logits[:p], bp, seq, cap_p)
    rs = _route_requests(logits[p:], bs, dec_seq, cap_s)
    ip = rp[..., 0].astype(jnp.int32) + (jnp.arange(bp, dtype=jnp.int32) * seq)[:, None, None]
    is_ = rs[..., 0].astype(jnp.int32) + (p + jnp.arange(bs, dtype=jnp.int32) * dec_seq)[:, None, None]
    idx = jnp.concatenate([jnp.swapaxes(ip, 0, 1).reshape(e, bp * cap_p),
                           jnp.swapaxes(is_, 0, 1).reshape(e, bs * cap_s)], axis=1)
    gate = jnp.concatenate([jnp.swapaxes(rp[..., 1], 0, 1).reshape(e, bp * cap_p),
                            jnp.swapaxes(rs[..., 1], 0, 1).reshape(e, bs * cap_s)], axis=1)
    return idx, gate


def _moe(x, norm_g, mods, layer, w_router_pad, w_gate, w_up, w_down, dims):
    e = N_EXPERTS
    h2, logits = _h2_router(x, norm_g, mods, layer, w_router_pad, dims)
    idx, gate = _route(logits, dims)
    return _moe_fused(x, h2, idx.reshape(-1), gate.reshape(-1, 1), w_gate, w_up, w_down, mods, layer, dims)


def _rope_tables(dims, dh):
    n, p = dims["dec_seq"], dims["P"]
    axis_dim = dh // 2
    tpos = jnp.arange(n)
    rowp = (tpos // GRID_W).astype(F32)
    colp = (tpos % GRID_W).astype(F32)
    inv = ROPE_BASE ** (-jnp.arange(0, axis_dim, 2, dtype=F32) / axis_dim)
    ar = rowp[:, None] * inv
    ac = colp[:, None] * inv
    cos = jnp.concatenate([jnp.cos(ar), jnp.cos(ar), jnp.cos(ac), jnp.cos(ac)], axis=1)
    sin = jnp.concatenate([-jnp.sin(ar), jnp.sin(ar), -jnp.sin(ac), jnp.sin(ac)], axis=1)
    cos = jnp.concatenate([jnp.ones((p, dh), F32), jnp.tile(cos, (dims["Bs"], 1))], axis=0)
    sin = jnp.concatenate([jnp.zeros((p, dh), F32), jnp.tile(sin, (dims["Bs"], 1))], axis=0)
    return cos, sin


def kernel(x_prompt, x_sample, state_mlstm_C, state_mlstm_n, state_mlstm_m, cache_attn_k, cache_attn_v,
           c, c_ctx, w_mod, b_mod, norm1_g, norm2_g, m_w_in, m_b_gate, m_norm_g, m_w_out,
           a_w_in, a_q_norm_g, a_k_norm_g, a_sink, a_w_out, e_w_router, e_w_gate, e_w_up, e_w_down):
    bp, seq, d = x_prompt.shape
    bs, dec_seq, _ = x_sample.shape
    depth = w_mod.shape[0]
    dims = dict(Bp=bp, Bs=bs, seq=seq, dec_seq=dec_seq, P=bp * seq, S=bs * dec_seq, D=d)
    nh = MLSTM_HEADS
    dv = d // nh
    dk = dv // 2
    qk, vw = nh * dk, nh * dv
    dh = d // ATTN_HEADS

    x = jnp.concatenate([x_prompt.reshape(bp * seq, d), x_sample.reshape(bs * dec_seq, d)], axis=0)

    cvecs = jnp.concatenate([c_ctx[None, :], c, jnp.zeros((N_GROUPS_PAD - 1 - bs, d), F32)], axis=0)
    mods = _adaln(cvecs, w_mod, b_mod).reshape(depth, N_GROUPS_PAD, N_MOD, 1, d)
    cos_t, sin_t = _rope_tables(dims, dh)

    new_c, new_n, new_m, new_k, new_v = [], [], [], [], []
    for l in range(depth):
        j = l // 2
        if l % 2 == 0:
            w_in = m_w_in[j]
            w_main = w_in[:, :2 * qk + 2 * vw].astype(BF16)
            w_gate = jnp.pad(w_in[:, 2 * qk + 2 * vw:], ((0, 0), (0, LANES - 4 * nh))).astype(BF16)
            b_gate = jnp.pad(m_b_gate[j], (0, LANES - 4 * nh)).reshape(1, LANES)
            main, gates = _mlstm_inproj(x, norm1_g[l].reshape(1, d), mods, l, w_main, w_gate, b_gate, dims)
            c0 = jnp.concatenate([jnp.zeros((bp, 2, nh, dk, dv), F32), state_mlstm_C[:, j]], axis=0)
            n0 = jnp.concatenate([jnp.zeros((bp, 2 * nh, dk), F32),
                                  state_mlstm_n[:, j].reshape(bs, 2 * nh, dk)], axis=0)
            m0 = jnp.concatenate([jnp.zeros((bp, 2 * nh), F32), state_mlstm_m[:, j].reshape(bs, 2 * nh)], axis=0)
            m0 = jnp.broadcast_to(m0[:, :, None], (bp + bs, 2 * nh, LANES))
            hf, hb, cfin, nfin, mfin = _mlstm_scan(main, gates, c0, n0, m0, dims)
            new_c.append(cfin[:bp])
            new_n.append(nfin[:bp].reshape(bp, 2, nh, dk))
            new_m.append(mfin[:bp, :, 0].reshape(bp, 2, nh))
            x = _mlstm_out(x, hf, hb, main, m_norm_g[j].reshape(1, vw), m_w_out[j].astype(BF16), mods, l, dims)
        else:
            qh, kvh = _attn_inproj(x, norm1_g[l].reshape(1, d), mods, l, a_w_in[j].astype(BF16),
                                   a_q_norm_g[j].reshape(1, dh), a_k_norm_g[j].reshape(1, dh), cos_t, sin_t, dims)
            kvw = ATTN_KV_HEADS * dh
            new_k.append(kvh[:bp * seq, :kvw].reshape(bp, seq, ATTN_KV_HEADS, dh).transpose(0, 2, 1, 3))
            new_v.append(kvh[:bp * seq, kvw:].reshape(bp, seq, ATTN_KV_HEADS, dh).transpose(0, 2, 1, 3))
            o = _attention(qh, kvh, cache_attn_k[:, j], cache_attn_v[:, j], a_sink[j], dims)
            x = _proj_res(x, o, a_w_out[j].astype(BF16), mods, l, 2, dims)
        w_router_pad = jnp.pad(e_w_router[l], ((0, 0), (0, LANES - N_EXPERTS))).astype(BF16)
        x = _moe(x, norm2_g[l].reshape(1, d), mods, l, w_router_pad, e_w_gate, e_w_up, e_w_down, dims)

    p = bp * seq
    return (x[:p].reshape(bp, seq, d), x[p:].reshape(bs, dec_seq, d),
            jnp.stack(new_c, axis=1), jnp.stack(new_n, axis=1), jnp.stack(new_m, axis=1),
            jnp.stack(new_k, axis=1), jnp.stack(new_v, axis=1))
```

```python
import functools

import jax
import jax.numpy as jnp
import numpy as np
from jax import lax
from jax.experimental import pallas as pl
from jax.experimental.pallas import tpu as pltpu

F32 = jnp.float32
BF16 = jnp.bfloat16

GRID_W = 64
MLSTM_HEADS = 8
ATTN_HEADS = 16
ATTN_KV_HEADS = 4
ATTN_GROUP = ATTN_HEADS // ATTN_KV_HEADS
WINDOW = 128
ROPE_BASE = 10000.0
N_EXPERTS = 16
CAPACITY_FACTOR = 2
N_MOD = 6
EPS = 1e-6
N_GROUPS_PAD = 8
LANES = 128
MLSTM_CHUNK = 256
ATTN_BLOCK = 128
VMEM_LIMIT = 56 * 1024 * 1024


def _cparams(sem):
    return pltpu.CompilerParams(dimension_semantics=sem, vmem_limit_bytes=VMEM_LIMIT)


def _dot(a, b):
    return jnp.dot(a, b, preferred_element_type=F32)


def _dot_nt(a, b):
    return lax.dot_general(a, b, (((1,), (1,)), ((), ())), preferred_element_type=F32)


def _group_of_block(i, rows_per_block, n_prompt, dec_seq):
    row = i * rows_per_block
    return jnp.where(row < n_prompt, 0, 1 + (row - n_prompt) // dec_seq)


def _norm_mod(x, g, shift, scale):
    y = x * lax.rsqrt(jnp.mean(x * x, axis=-1, keepdims=True) + EPS)
    return (y * g) * (1.0 + scale) + shift


def _log_sigmoid(x):
    return jnp.minimum(x, 0.0) - jnp.log(1.0 + jnp.exp(-jnp.abs(x)))


def _split3(x):
    hi = x.astype(BF16)
    r1 = x - hi.astype(F32)
    mid = r1.astype(BF16)
    lo = (r1 - mid.astype(F32)).astype(BF16)
    return hi, mid, lo


def _adaln_kernel(c_ref, w_ref, b_ref, o_ref):
    c = c_ref[...]
    s = (c * jax.nn.sigmoid(c)).astype(BF16)
    o_ref[...] = _dot(s, w_ref[...].astype(BF16)) + b_ref[...]


def _adaln(cvecs, w_mod, b_mod):
    depth, d, n = w_mod.shape
    tn = 1024
    return pl.pallas_call(
        _adaln_kernel,
        out_shape=jax.ShapeDtypeStruct((depth, N_GROUPS_PAD, n), F32),
        grid=(depth, n // tn),
        in_specs=[
            pl.BlockSpec((N_GROUPS_PAD, d), lambda l, j: (0, 0)),
            pl.BlockSpec((None, d, tn), lambda l, j: (l, 0, j)),
            pl.BlockSpec((None, 1, tn), lambda l, j: (l, 0, j)),
        ],
        out_specs=pl.BlockSpec((None, N_GROUPS_PAD, tn), lambda l, j: (l, 0, j)),
        compiler_params=_cparams(("parallel", "parallel")),
        name="adaln",
    )(cvecs, w_mod, b_mod.reshape(depth, 1, n))


def _mod_spec(layer, which, tm, n_prompt, dec_seq, d, ncols=None, with_j=False):
    if ncols is None:
        ncols = d
    if with_j:
        return pl.BlockSpec((None, None, None, 1, ncols),
                            lambda i, j: (layer, _group_of_block(i, tm, n_prompt, dec_seq), which, 0, 0))
    return pl.BlockSpec((None, None, None, 1, ncols),
                        lambda i: (layer, _group_of_block(i, tm, n_prompt, dec_seq), which, 0, 0))


def _mlstm_inproj_kernel(x_ref, g_ref, sh_ref, sc_ref, w_ref, wg_ref, bg_ref, o_ref, og_ref, h_scr,
                         *, nq_blocks, qscale):
    j = pl.program_id(1)

    @pl.when(j == 0)
    def _():
        h = _norm_mod(x_ref[...], g_ref[...], sh_ref[...], sc_ref[...]).astype(BF16)
        h_scr[...] = h
        og_ref[...] = _dot(h, wg_ref[...]) + bg_ref[...]

    acc = _dot(h_scr[...], w_ref[...])
    scale = jnp.where(j < nq_blocks, qscale, 1.0).astype(F32)
    o_ref[...] = (acc * scale).astype(o_ref.dtype)


def _mlstm_inproj(x, norm_g, mods, layer, w_main, w_gate, b_gate, dims):
    t, d = x.shape
    n = w_main.shape[1]
    tm, tn = 1024, 512
    dk = d // MLSTM_HEADS // 2
    kern = functools.partial(_mlstm_inproj_kernel, nq_blocks=(MLSTM_HEADS * dk) // tn, qscale=float(dk) ** -0.5)
    return pl.pallas_call(
        kern,
        out_shape=(jax.ShapeDtypeStruct((t, n), BF16), jax.ShapeDtypeStruct((t, LANES), F32)),
        grid=(t // tm, n // tn),
        in_specs=[
            pl.BlockSpec((tm, d), lambda i, j: (i, 0)),
            pl.BlockSpec((1, d), lambda i, j: (0, 0)),
            _mod_spec(layer, 0, tm, dims["P"], dims["dec_seq"], d, with_j=True),
            _mod_spec(layer, 1, tm, dims["P"], dims["dec_seq"], d, with_j=True),
            pl.BlockSpec((d, tn), lambda i, j: (0, j)),
            pl.BlockSpec((d, LANES), lambda i, j: (0, 0)),
            pl.BlockSpec((1, LANES), lambda i, j: (0, 0)),
        ],
        out_specs=(pl.BlockSpec((tm, tn), lambda i, j: (i, j)),
                   pl.BlockSpec((tm, LANES), lambda i, j: (i, 0))),
        scratch_shapes=[pltpu.VMEM((tm, d), BF16)],
        compiler_params=_cparams(("parallel", "arbitrary")),
        name="mlstm_inproj",
    )(x, norm_g, mods, mods, w_main, w_gate, b_gate)


def _mlstm_kernel(fblk, bblk, seq, first, last,
                  qf, kf, vf, gf, qb, kb, vb, gb, c0, n0, m0,
                  hf, hb, cfin, nfin, mfin, c_scr, n_scr, m_scr, *, L, NH, DK, DV, NPROMPT):
    s = pl.program_id(0)

    @pl.when((first[s] == 1) & (seq[s] < NPROMPT))
    def _():
        c_scr[...] = jnp.zeros_like(c_scr)
        n_scr[...] = jnp.zeros_like(n_scr)
        m_scr[...] = jnp.zeros_like(m_scr)

    @pl.when((first[s] == 1) & (seq[s] >= NPROMPT))
    def _():
        c_scr[...] = c0[...]
        n_scr[...] = n0[...]
        m_scr[...] = m0[...]

    row = lax.broadcasted_iota(jnp.int32, (L, L), 0)
    col = lax.broadcasted_iota(jnp.int32, (L, L), 1)
    tril = col <= row
    triu = col >= row
    tril_bf = tril.astype(BF16)
    triu_bf = triu.astype(BF16)

    dirs = ((qf, kf, vf, gf, hf, tril, tril_bf, triu_bf, L - 1),
            (qb, kb, vb, gb, hb, triu, triu_bf, tril_bf, 0))
    for d, (q_ref, k_ref, v_ref, g_ref, h_ref, keep, keep_bf, keep_t_bf, last_row) in enumerate(dirs):
        g = g_ref[...]
        lf = _log_sigmoid(g)
        l1, l2, l3 = _split3(lf)
        bcol = _dot(keep_bf, l1) + _dot(keep_bf, l2) + _dot(keep_bf, l3)
        g_t = g.T
        lf_t = _log_sigmoid(g_t)
        t1, t2, t3 = _split3(lf_t)
        brow = _dot(t1, keep_t_bf) + _dot(t2, keep_t_bf) + _dot(t3, keep_t_bf)
        base = 2 * NH * d
        for h in range(NH):
            ci = base + h
            cf = base + NH + h
            sr = NH * d + h
            li_c = g[:, ci:ci + 1]
            b_c = bcol[:, cf:cf + 1]
            li_r = g_t[ci:ci + 1, :]
            b_r = brow[cf:cf + 1, :]
            m_prev = m_scr[sr:sr + 1, 0:1]
            dmat = jnp.where(keep, (b_c - b_r) + li_r, -jnp.inf)
            gv = b_c + m_prev
            m_t = jnp.maximum(gv, jnp.max(dmat, axis=1, keepdims=True))
            w_inter = jnp.exp(gv - m_t)
            w_intra = jnp.exp(dmat - m_t)
            q = q_ref[:, h * DK:(h + 1) * DK]
            k = k_ref[:, h * DK:(h + 1) * DK]
            v = v_ref[:, h * DV:(h + 1) * DV]
            sc = _dot_nt(q, k) * w_intra
            c_st = c_scr[d, h]
            n_st = n_scr[sr:sr + 1, :]
            num = w_inter * _dot(q, c_st.astype(BF16)) + _dot(sc.astype(BF16), v)
            den = (w_inter * jnp.sum(q.astype(F32) * n_st, axis=1, keepdims=True)
                   + jnp.sum(sc, axis=1, keepdims=True))
            h_ref[:, h * DV:(h + 1) * DV] = num / jnp.maximum(jnp.abs(den), jnp.exp(-m_t))
            b_last = b_c[last_row:last_row + 1, :]
            a = (b_last - b_c) + li_c
            m_loc = jnp.max(a, axis=0, keepdims=True)
            w = jnp.exp(a - m_loc)
            kw = k.astype(F32) * w
            c_loc = _dot(kw.T.astype(BF16), v)
            n_loc = jnp.sum(kw, axis=0, keepdims=True)
            m_new = jnp.maximum(b_last + m_prev, m_loc)
            w_old = jnp.exp(b_last + m_prev - m_new)
            w_new = jnp.exp(m_loc - m_new)
            c_scr[d, h] = w_old * c_st + w_new * c_loc
            n_scr[sr:sr + 1, :] = w_old * n_st + w_new * n_loc
            m_scr[sr:sr + 1, :] = jnp.broadcast_to(m_new, (1, LANES))

    @pl.when(last[s] == 1)
    def _():
        cfin[...] = c_scr[...]
        nfin[...] = n_scr[...]
        mfin[...] = m_scr[...]


def _mlstm_scan(main, gates, c0, n0, m0, dims):
    t = main.shape[0]
    d_model = dims["D"]
    nh = MLSTM_HEADS
    dv = d_model // nh
    dk = dv // 2
    L = MLSTM_CHUNK
    assert dims["seq"] == L and dims["dec_seq"] % L == 0
    bp, bs = dims["Bp"], dims["Bs"]
    ncs = dims["dec_seq"] // L
    pblk = dims["P"] // L
    fblk = np.concatenate([np.arange(bp), pblk + np.arange(bs * ncs)]).astype(np.int32)
    bblk = np.concatenate([np.arange(bp),
                           pblk + (np.arange(bs)[:, None] * ncs + (ncs - 1 - np.arange(ncs))[None, :]).reshape(-1)]
                          ).astype(np.int32)
    seq = np.concatenate([np.arange(bp), bp + np.repeat(np.arange(bs), ncs)]).astype(np.int32)
    first = np.concatenate([np.ones(bp), (np.tile(np.arange(ncs), bs) == 0)]).astype(np.int32)
    last = np.concatenate([np.ones(bp), (np.tile(np.arange(ncs), bs) == ncs - 1)]).astype(np.int32)
    nseq = bp + bs
    nsteps = fblk.shape[0]
    qk = nh * dk
    vw = nh * dv
    kern = functools.partial(_mlstm_kernel, L=L, NH=nh, DK=dk, DV=dv, NPROMPT=bp)

    def fmap(cb):
        return lambda s, fb, bb, sq, fi, la: (fb[s], cb)

    def bmap(cb):
        return lambda s, fb, bb, sq, fi, la: (bb[s], cb)

    def smap(nd):
        return lambda s, fb, bb, sq, fi, la: (sq[s],) + (0,) * nd

    def imap(nd):
        return lambda s, fb, bb, sq, fi, la: (jnp.maximum(sq[s] - bp, 0),) + (0,) * nd

    grid_spec = pltpu.PrefetchScalarGridSpec(
        num_scalar_prefetch=5,
        grid=(nsteps,),
        in_specs=[
            pl.BlockSpec((L, qk), fmap(0)), pl.BlockSpec((L, qk), fmap(1)),
            pl.BlockSpec((L, vw), fmap(2 * qk // vw)), pl.BlockSpec((L, LANES), fmap(0)),
            pl.BlockSpec((L, qk), bmap(0)), pl.BlockSpec((L, qk), bmap(1)),
            pl.BlockSpec((L, vw), bmap(2 * qk // vw)), pl.BlockSpec((L, LANES), bmap(0)),
            pl.BlockSpec((None, 2, nh, dk, dv), imap(4)),
            pl.BlockSpec((None, 2 * nh, dk), imap(2)),
            pl.BlockSpec((None, 2 * nh, LANES), imap(2)),
        ],
        out_specs=(
            pl.BlockSpec((L, vw), fmap(0)), pl.BlockSpec((L, vw), bmap(0)),
            pl.BlockSpec((None, 2, nh, dk, dv), smap(4)),
            pl.BlockSpec((None, 2 * nh, dk), smap(2)),
            pl.BlockSpec((None, 2 * nh, LANES), smap(2)),
        ),
        scratch_shapes=[pltpu.VMEM((2, nh, dk, dv), F32), pltpu.VMEM((2 * nh, dk), F32),
                        pltpu.VMEM((2 * nh, LANES), F32)],
    )
    return pl.pallas_call(
        kern,
        out_shape=(jax.ShapeDtypeStruct((t, vw), F32), jax.ShapeDtypeStruct((t, vw), F32),
                   jax.ShapeDtypeStruct((nseq, 2, nh, dk, dv), F32),
                   jax.ShapeDtypeStruct((nseq, 2 * nh, dk), F32),
                   jax.ShapeDtypeStruct((nseq, 2 * nh, LANES), F32)),
        grid_spec=grid_spec,
        compiler_params=_cparams(("arbitrary",)),
        name="mlstm_scan",
    )(jnp.asarray(fblk), jnp.asarray(bblk), jnp.asarray(seq), jnp.asarray(first), jnp.asarray(last),
      main, main, main, gates, main, main, main, gates, c0, n0, m0)


def _mlstm_out_kernel(x_ref, hf_ref, hb_ref, o_ref, ng_ref, w_ref, g1_ref, out_ref, *, NH, DV):
    parts = []
    for h in range(NH):
        sl = slice(h * DV, (h + 1) * DV)
        hs = hf_ref[:, sl] + hb_ref[:, sl]
        hs = hs * lax.rsqrt(jnp.mean(hs * hs, axis=-1, keepdims=True) + EPS)
        hs = hs * ng_ref[:, sl]
        parts.append((hs * jax.nn.sigmoid(o_ref[:, sl].astype(F32))).astype(BF16))
    z = jnp.concatenate(parts, axis=1)
    out_ref[...] = x_ref[...] + g1_ref[...] * _dot(z, w_ref[...])


def _mlstm_out(x, hf, hb, main, norm_g, w_out, mods, layer, dims):
    t, d = x.shape
    tm = 256
    vw = hf.shape[1]
    kern = functools.partial(_mlstm_out_kernel, NH=MLSTM_HEADS, DV=vw // MLSTM_HEADS)
    ocol = (main.shape[1] - vw) // vw
    return pl.pallas_call(
        kern,
        out_shape=jax.ShapeDtypeStruct((t, d), F32),
        grid=(t // tm,),
        in_specs=[
            pl.BlockSpec((tm, d), lambda i: (i, 0)),
            pl.BlockSpec((tm, vw), lambda i: (i, 0)),
            pl.BlockSpec((tm, vw), lambda i: (i, 0)),
            pl.BlockSpec((tm, vw), lambda i: (i, ocol)),
            pl.BlockSpec((1, vw), lambda i: (0, 0)),
            pl.BlockSpec((vw, d), lambda i: (0, 0)),
            _mod_spec(layer, 2, tm, dims["P"], dims["dec_seq"], d),
        ],
        out_specs=pl.BlockSpec((tm, d), lambda i: (i, 0)),
        compiler_params=_cparams(("parallel",)),
        name="mlstm_out",
    )(x, hf, hb, main, norm_g, w_out, mods)


def _attn_inproj_kernel(x_ref, g_ref, sh_ref, sc_ref, w_ref, qn_ref, kn_ref, cos_ref, sin_ref,
                        oq_ref, okv_ref, h_scr, *, nq_blocks, heads_per_block, DH):
    j = pl.program_id(1)

    @pl.when(j == 0)
    def _():
        h_scr[...] = _norm_mod(x_ref[...], g_ref[...], sh_ref[...], sc_ref[...]).astype(BF16)

    acc = _dot(h_scr[...], w_ref[...])

    def normed(gain):
        cos = cos_ref[...]
        sin = sin_ref[...]
        lane = lax.broadcasted_iota(jnp.int32, cos.shape, 1)
        first_half = (lane % (DH // 2)) < (DH // 4)
        outs = []
        for hh in range(heads_per_block):
            a = acc[:, hh * DH:(hh + 1) * DH]
            r = a * lax.rsqrt(jnp.mean(a * a, axis=-1, keepdims=True) + EPS) * gain
            partner = jnp.where(first_half, pltpu.roll(r, DH - DH // 4, 1), pltpu.roll(r, DH // 4, 1))
            outs.append(r * cos + partner * sin)
        return jnp.concatenate(outs, axis=1)

    @pl.when(j < nq_blocks)
    def _():
        oq_ref[...] = normed(qn_ref[...]).astype(oq_ref.dtype)

    @pl.when(j == nq_blocks)
    def _():
        okv_ref[...] = normed(kn_ref[...])

    @pl.when(j > nq_blocks)
    def _():
        okv_ref[...] = acc


def _attn_inproj(x, norm_g, mods, layer, w_in, qn_g, kn_g, cos_t, sin_t, dims):
    t, d = x.shape
    n = w_in.shape[1]
    dh = d // ATTN_HEADS
    tm, tn = 1024, 512
    assert ATTN_KV_HEADS * dh == tn
    nqb = (ATTN_HEADS * dh) // tn
    kern = functools.partial(_attn_inproj_kernel, nq_blocks=nqb, heads_per_block=tn // dh, DH=dh)
    return pl.pallas_call(
        kern,
        out_shape=(jax.ShapeDtypeStruct((t, ATTN_HEADS * dh), BF16),
                   jax.ShapeDtypeStruct((t, 2 * ATTN_KV_HEADS * dh), F32)),
        grid=(t // tm, n // tn),
        in_specs=[
            pl.BlockSpec((tm, d), lambda i, j: (i, 0)),
            pl.BlockSpec((1, d), lambda i, j: (0, 0)),
            _mod_spec(layer, 0, tm, dims["P"], dims["dec_seq"], d, with_j=True),
            _mod_spec(layer, 1, tm, dims["P"], dims["dec_seq"], d, with_j=True),
            pl.BlockSpec((d, tn), lambda i, j: (0, j)),
            pl.BlockSpec((1, dh), lambda i, j: (0, 0)),
            pl.BlockSpec((1, dh), lambda i, j: (0, 0)),
            pl.BlockSpec((tm, dh), lambda i, j: (i, 0)),
            pl.BlockSpec((tm, dh), lambda i, j: (i, 0)),
        ],
        out_specs=(pl.BlockSpec((tm, tn), lambda i, j: (i, jnp.minimum(j, nqb - 1))),
                   pl.BlockSpec((tm, tn), lambda i, j: (i, jnp.maximum(j - nqb, 0)))),
        scratch_shapes=[pltpu.VMEM((tm, d), BF16)],
        compiler_params=_cparams(("parallel", "arbitrary")),
        name="attn_inproj",
    )(x, norm_g, mods, mods, w_in, qn_g, kn_g, cos_t, sin_t)


def _softmax_sink_rows(s_list, sink_col):
    m = sink_col
    for s in s_list:
        m = jnp.maximum(m, jnp.max(s, axis=1, keepdims=True))
    ps = [jnp.exp(s - m) for s in s_list]
    tot = jnp.exp(sink_col - m)
    for p in ps:
        tot = tot + jnp.sum(p, axis=1, keepdims=True)
    return ps, tot


def _ctx_attn_kernel(sink_ref, q_ref, k_ref, v_ref, o_ref, *, G, DH):
    kv = pl.program_id(1)
    k = k_ref[...].astype(BF16)
    v = v_ref[...].astype(BF16)
    scale = float(DH) ** -0.5
    for g in range(G):
        q = q_ref[:, g * DH:(g + 1) * DH]
        s = _dot_nt(q, k) * scale
        sink = jnp.full((s.shape[0], 1), sink_ref[kv * G + g], F32)
        (p,), tot = _softmax_sink_rows([s], sink)
        o_ref[:, g * DH:(g + 1) * DH] = (_dot(p.astype(BF16), v) / tot).astype(o_ref.dtype)


def _lat_attn_kernel(sink_ref, q_ref, kp_ref, kc_ref, kn_ref, vp_ref, vc_ref, vn_ref, kx_ref, vx_ref, o_ref,
                     *, G, DH, NB, BLK):
    kv = pl.program_id(1)
    i = pl.program_id(2)
    scale = float(DH) ** -0.5
    q = jnp.concatenate([q_ref[:, g * DH:(g + 1) * DH] for g in range(G)], axis=0)
    rows = G * BLK
    r = lax.broadcasted_iota(jnp.int32, (rows, BLK), 0) % BLK
    c = lax.broadcasted_iota(jnp.int32, (rows, BLK), 1)
    s_p = jnp.where((c >= r) & (i > 0), _dot_nt(q, kp_ref[...].astype(BF16)) * scale, -jnp.inf)
    s_m = _dot_nt(q, kc_ref[...].astype(BF16)) * scale
    s_n = jnp.where((c <= r) & (i < NB - 1), _dot_nt(q, kn_ref[...].astype(BF16)) * scale, -jnp.inf)
    s_x = _dot_nt(q, kx_ref[...].astype(BF16)) * scale
    rid = lax.broadcasted_iota(jnp.int32, (rows, 1), 0) // BLK
    sink = jnp.zeros((rows, 1), F32)
    for g in range(G):
        sink = jnp.where(rid == g, sink_ref[kv * G + g], sink)
    (p_p, p_m, p_n, p_x), tot = _softmax_sink_rows([s_p, s_m, s_n, s_x], sink)
    o = (_dot(p_p.astype(BF16), vp_ref[...].astype(BF16)) + _dot(p_m.astype(BF16), vc_ref[...].astype(BF16))
         + _dot(p_n.astype(BF16), vn_ref[...].astype(BF16)) + _dot(p_x.astype(BF16), vx_ref[...].astype(BF16))) / tot
    for g in range(G):
        o_ref[:, g * DH:(g + 1) * DH] = o[g * BLK:(g + 1) * BLK, :].astype(o_ref.dtype)


def _attention(qh, kvh, cache_k, cache_v, sink, dims):
    t = qh.shape[0]
    dh = qh.shape[1] // ATTN_HEADS
    G, KV = ATTN_GROUP, ATTN_KV_HEADS
    bp, bs, seq, dec_seq = dims["Bp"], dims["Bs"], dims["seq"], dims["dec_seq"]
    gw = G * dh
    o_ctx = pl.pallas_call(
        functools.partial(_ctx_attn_kernel, G=G, DH=dh),
        out_shape=jax.ShapeDtypeStruct((dims["P"], ATTN_HEADS * dh), BF16),
        grid_spec=pltpu.PrefetchScalarGridSpec(
            num_scalar_prefetch=1,
            grid=(bp, KV),
            in_specs=[
                pl.BlockSpec((seq, gw), lambda b, kv, sk: (b, kv)),
                pl.BlockSpec((seq, dh), lambda b, kv, sk: (b, kv)),
                pl.BlockSpec((seq, dh), lambda b, kv, sk: (b, KV + kv)),
            ],
            out_specs=pl.BlockSpec((seq, gw), lambda b, kv, sk: (b, kv)),
        ),
        compiler_params=_cparams(("parallel", "parallel")),
        name="ctx_attn",
    )(sink, qh, kvh, kvh)
    blk = ATTN_BLOCK
    assert blk == WINDOW and dec_seq % blk == 0 and dims["P"] % blk == 0
    nb = dec_seq // blk
    off = dims["P"] // blk

    def qmap(r, kv, i, sk):
        return (off + r * nb + i, kv)

    def omap(r, kv, i, sk):
        return (r * nb + i, kv)

    def kmap(delta, colbase):
        def f(r, kv, i, sk):
            return (off + r * nb + jnp.clip(i + delta, 0, nb - 1), colbase + kv)
        return f

    o_lat = pl.pallas_call(
        functools.partial(_lat_attn_kernel, G=G, DH=dh, NB=nb, BLK=blk),
        out_shape=jax.ShapeDtypeStruct((dims["S"], ATTN_HEADS * dh), BF16),
        grid_spec=pltpu.PrefetchScalarGridSpec(
            num_scalar_prefetch=1,
            grid=(bs, KV, nb),
            in_specs=[
                pl.BlockSpec((blk, gw), qmap),
                pl.BlockSpec((blk, dh), kmap(-1, 0)), pl.BlockSpec((blk, dh), kmap(0, 0)),
                pl.BlockSpec((blk, dh), kmap(1, 0)),
                pl.BlockSpec((blk, dh), kmap(-1, KV)), pl.BlockSpec((blk, dh), kmap(0, KV)),
                pl.BlockSpec((blk, dh), kmap(1, KV)),
                pl.BlockSpec((None, None, cache_k.shape[2], dh), lambda r, kv, i, sk: (r, kv, 0, 0)),
                pl.BlockSpec((None, None, cache_v.shape[2], dh), lambda r, kv, i, sk: (r, kv, 0, 0)),
            ],
            out_specs=pl.BlockSpec((blk, gw), omap),
        ),
        compiler_params=_cparams(("parallel", "parallel", "arbitrary")),
        name="lat_attn",
    )(sink, qh, kvh, kvh, kvh, kvh, kvh, kvh, cache_k, cache_v)
    return o_ctx, o_lat


def _proj_res_kernel(x_ref, ap_ref, as_ref, w_ref, g1_ref, out_ref, *, P_TILES):
    i = pl.program_id(0)

    @pl.when(i < P_TILES)
    def _():
        out_ref[...] = x_ref[...] + g1_ref[...] * _dot(ap_ref[...], w_ref[...])

    @pl.when(i >= P_TILES)
    def _():
        out_ref[...] = x_ref[...] + g1_ref[...] * _dot(as_ref[...], w_ref[...])


def _proj_res(x, a_prompt, a_latent, w, mods, layer, which, dims):
    t, d = x.shape
    tm = 512
    pt = dims["P"] // tm
    assert dims["P"] % tm == 0 and dims["S"] % tm == 0
    return pl.pallas_call(
        functools.partial(_proj_res_kernel, P_TILES=pt),
        out_shape=jax.ShapeDtypeStruct((t, d), F32),
        grid=(t // tm,),
        in_specs=[
            pl.BlockSpec((tm, d), lambda i: (i, 0)),
            pl.BlockSpec((tm, a_prompt.shape[1]), lambda i: (jnp.minimum(i, pt - 1), 0)),
            pl.BlockSpec((tm, a_latent.shape[1]), lambda i: (jnp.maximum(i - pt, 0), 0)),
            pl.BlockSpec(w.shape, lambda i: (0, 0)),
            _mod_spec(layer, which, tm, dims["P"], dims["dec_seq"], d),
        ],
        out_specs=pl.BlockSpec((tm, d), lambda i: (i, 0)),
        compiler_params=_cparams(("parallel",)),
        name="proj_res",
    )(x, a_prompt, a_latent, w, mods)


def _h2_router_kernel(x_ref, g_ref, sh_ref, sc_ref, wr_ref, h_ref, lg_ref):
    hb = _norm_mod(x_ref[...], g_ref[...], sh_ref[...], sc_ref[...]).astype(BF16)
    lg_ref[...] = _dot(hb, wr_ref[...])
    bits = pltpu.bitcast(hb.astype(F32), jnp.int32)
    half = bits.shape[1] // 2
    lo = bits[:, :half]
    h_ref[...] = (bits[:, half:] & jnp.int32(-65536)) | lax.shift_right_logical(lo, jnp.full(lo.shape, 16, jnp.int32))


def _h2_router(x, norm_g, mods, layer, w_router_pad, dims):
    t, d = x.shape
    tm = 512
    return pl.pallas_call(
        _h2_router_kernel,
        out_shape=(jax.ShapeDtypeStruct((t, d // 2), jnp.int32), jax.ShapeDtypeStruct((t, LANES), F32)),
        grid=(t // tm,),
        in_specs=[
            pl.BlockSpec((tm, d), lambda i: (i, 0)),
            pl.BlockSpec((1, d), lambda i: (0, 0)),
            _mod_spec(layer, 3, tm, dims["P"], dims["dec_seq"], d),
            _mod_spec(layer, 4, tm, dims["P"], dims["dec_seq"], d),
            pl.BlockSpec((d, LANES), lambda i: (0, 0)),
        ],
        out_specs=(pl.BlockSpec((tm, d // 2), lambda i: (i, 0)), pl.BlockSpec((tm, LANES), lambda i: (i, 0))),
        compiler_params=_cparams(("parallel",)),
        name="h2_router",
    )(x, norm_g, mods, mods, w_router_pad)


def _moe_kernel(idx_ref, h2_hbm, wg_ref, wu_ref, wd_ref, gate_ref, *rest, TM, TF, NF, MT, NTILES, NSUB):
    g2_refs = rest[:NSUB]
    x_hbm, o_hbm, xf_ref, xb_ref, hm_ref, acc_ref, sem_x, sem_acc, sem_out = rest[NSUB:]
    del x_hbm
    s = pl.program_id(2)
    n = pl.program_id(0) * MT + pl.program_id(1)
    base = n * TM
    nh = NF // 2
    chunk_h2 = TM // NF
    chunk_rw = TM // nh
    sub = TM // NSUB
    dhalf = xf_ref.shape[1]

    def h2_wait():
        pltpu.make_async_copy(h2_hbm.at[pl.ds(0, TM), :], xf_ref, sem_x).wait()

    def stream_wait(sem):
        pltpu.make_async_copy(o_hbm.at[pl.ds(0, TM), :], acc_ref, sem).wait()

    def h2_copy(tile_base, r):
        tok = idx_ref[tile_base + r]
        return pltpu.make_async_copy(h2_hbm.at[pl.ds(tok, 1), :], xf_ref.at[pl.ds(r, 1), :], sem_x)

    def out_copy(tile_base, r):
        tok = idx_ref[tile_base + r]
        return pltpu.make_async_copy(acc_ref.at[pl.ds(r, 1), :], o_hbm.at[pl.ds(tok, 1), :], sem_out)

    def issue_h2(tile_base, r0, cnt):
        for r in range(cnt):
            h2_copy(tile_base, r0 + r).start(priority=r % 2)

    def issue_acc(r0, cnt):
        for r in range(cnt):
            tok = idx_ref[base + r0 + r]
            pltpu.make_async_copy(o_hbm.at[pl.ds(tok, 1), :], acc_ref.at[pl.ds(r0 + r, 1), :],
                                  sem_acc).start(priority=r % 2)

    def issue_out(tile_base, r0, cnt):
        for r in range(cnt):
            out_copy(tile_base, r0 + r).start(priority=r % 2)

    @pl.when((s == 0) & (n == 0))
    def _():
        def body(r, carry):
            h2_copy(0, r).start()
            return carry
        lax.fori_loop(0, TM, body, 0)

    @pl.when(s == 0)
    def _():
        h2_wait()
        w = xf_ref[...]
        xb_ref[:, 0:dhalf] = pltpu.bitcast(w << 16, F32).astype(BF16)
        xb_ref[:, dhalf:2 * dhalf] = pltpu.bitcast(w & jnp.int32(-65536), F32).astype(BF16)

    def hidden_block(issue, r0, cnt):
        wg = wg_ref[...].astype(BF16)
        wu = wu_ref[...].astype(BF16)
        part = cnt // NSUB
        for k in range(NSUB):
            rs = slice(k * sub, (k + 1) * sub)
            if issue is not None:
                issue(r0 + k * part, part)
            xb = xb_ref[rs, :]
            a = _dot(xb, wg)
            u = _dot(xb, wu)
            hm_ref[s, rs, :] = (a * jax.nn.sigmoid(a) * u).astype(BF16)

    @pl.when((s < nh) & (n == 0))
    def _():
        hidden_block(None, 0, 0)

    @pl.when((s < nh) & (n > 0))
    def _():
        hidden_block(functools.partial(issue_out, base - TM), s * chunk_rw, chunk_rw)

    @pl.when((s == nh) & (n > 0))
    def _():
        stream_wait(sem_out)

    @pl.when((s >= nh) & (s < NF))
    def _():
        hidden_block(issue_acc, (s - nh) * chunk_rw, chunk_rw)

    @pl.when(s == NF)
    def _():
        stream_wait(sem_acc)

    for c in range(NF):
        @pl.when(s == NF + c)
        def _(c=c):
            cols = slice(c * TF, (c + 1) * TF)
            wd = [wd_ref[f * TF:(f + 1) * TF, :].astype(BF16) for f in range(NF)]
            part = chunk_h2 // NSUB
            for k in range(NSUB):
                rs = slice(k * sub, (k + 1) * sub)
                issue_h2(base + TM, c * chunk_h2 + k * part, part)
                y = _dot(hm_ref[0, rs, :], wd[0])
                for f in range(1, NF):
                    y = y + _dot(hm_ref[f, rs, :], wd[f])
                acc_ref[rs, cols] = acc_ref[rs, cols] + (y * gate_ref[rs, :]) * g2_refs[k][:, cols]

    @pl.when((s == 2 * NF - 1) & (n == NTILES - 1))
    def _():
        def body(r, carry):
            out_copy(base, r).start()
            return carry
        lax.fori_loop(0, TM, body, 0)
        stream_wait(sem_out)
        h2_wait()


def _moe_fused(x, h2, idx_flat, gates_col, w_gate, w_up, w_down, mods, layer, dims):
    t, d = x.shape
    e = N_EXPERTS
    dff = w_gate.shape[-1]
    rows = idx_flat.shape[0] // e
    rows_p = dims["Bp"] * (CAPACITY_FACTOR * dims["seq"] // e)
    cap_s = CAPACITY_FACTOR * dims["dec_seq"] // e
    mt = 2
    tm = rows // mt
    sub = cap_s
    nsub = tm // sub
    assert rows % mt == 0 and tm % sub == 0 and rows_p % sub == 0 and rows_p <= tm
    tf = 256
    nf = dff // tf
    assert d == dff and nf % 2 == 0 and tm % (nf * nsub) == 0
    ntiles = e * mt
    idx_pad = jnp.concatenate([idx_flat, jnp.zeros((tm,), jnp.int32)])

    def g2_spec(k):
        def imap(ex, m, s, idx):
            r = m * tm + k * sub
            return (layer, jnp.where(r < rows_p, 0, 1 + (r - rows_p) // cap_s), 5, 0, 0)
        return pl.BlockSpec((None, None, None, 1, d), imap)

    kern = functools.partial(_moe_kernel, TM=tm, TF=tf, NF=nf, MT=mt, NTILES=ntiles, NSUB=nsub)
    return pl.pallas_call(
        kern,
        out_shape=jax.ShapeDtypeStruct((t, d), F32),
        grid_spec=pltpu.PrefetchScalarGridSpec(
            num_scalar_prefetch=1,
            grid=(e, mt, 2 * nf),
            in_specs=[
                pl.BlockSpec(memory_space=pl.ANY),
                pl.BlockSpec((None, None, d, tf), lambda ex, m, s, idx: (layer, ex, 0, jnp.minimum(s, nf - 1))),
                pl.BlockSpec((None, None, d, tf), lambda ex, m, s, idx: (layer, ex, 0, jnp.minimum(s, nf - 1))),
                pl.BlockSpec((None, None, dff, tf), lambda ex, m, s, idx: (layer, ex, 0, jnp.maximum(s - nf, 0))),
                pl.BlockSpec((tm, 1), lambda ex, m, s, idx: (ex * mt + m, 0)),
            ] + [g2_spec(k) for k in range(nsub)] + [pl.BlockSpec(memory_space=pl.ANY)],
            out_specs=pl.BlockSpec(memory_space=pl.ANY),
            scratch_shapes=[
                pltpu.VMEM((tm, d // 2), jnp.int32), pltpu.VMEM((tm, d), BF16), pltpu.VMEM((nf, tm, tf), BF16),
                pltpu.VMEM((tm, d), F32),
                pltpu.SemaphoreType.DMA, pltpu.SemaphoreType.DMA, pltpu.SemaphoreType.DMA,
            ],
        ),
        input_output_aliases={6 + nsub: 0},
        compiler_params=_cparams(("arbitrary", "arbitrary", "arbitrary")),
        name="moe_fused",
    )(idx_pad, h2, w_gate, w_up, w_down, gates_col, *([mods] * nsub), x)


ROUTE_LANE_BLOCK = 256
TOK_RADIX = 256


def _excl_cumsum_lanes(mask_f, upper_bf):
    n = mask_f.shape[1]
    b = upper_bf.shape[0]
    outs = []
    carry = jnp.zeros((mask_f.shape[0], 1), F32)
    for j in range(n // b):
        blk = mask_f[:, j * b:(j + 1) * b]
        outs.append(_dot(blk.astype(BF16), upper_bf) + carry)
        carry = carry + jnp.sum(blk, axis=1, keepdims=True)
    return jnp.concatenate(outs, axis=1)


def _route_kernel(lg_ref, o_ref, pos_scr, w_scr, *, N, CAP, E, RB, UNROLL):
    lg = lg_ref[...]
    lane = lax.broadcasted_iota(jnp.int32, lg.shape, 1)
    x = jnp.where(lane < E, lg, -jnp.inf)
    ex = jnp.exp(x - jnp.max(x, axis=1, keepdims=True))
    aff = ex / jnp.sum(ex, axis=1, keepdims=True)
    bits = pltpu.bitcast(jnp.concatenate([aff[q * N:(q + 1) * N, :].T[0:E, :] for q in range(RB)], axis=0), jnp.int32)

    def refine(i, prefix):
        cand = prefix | jnp.left_shift(jnp.int32(1), 30 - i)
        cnt = jnp.sum((bits >= cand).astype(F32), axis=1, keepdims=True)
        return jnp.where(cnt >= CAP, cand, prefix)

    thr = lax.fori_loop(0, 31, refine, jnp.zeros((RB * E, 1), jnp.int32))
    b = ROUTE_LANE_BLOCK
    r_i = lax.broadcasted_iota(jnp.int32, (b, b), 0)
    c_i = lax.broadcasted_iota(jnp.int32, (b, b), 1)
    upper = (r_i < c_i).astype(BF16)
    gt = bits > thr
    eq = bits == thr
    need = CAP - jnp.sum(gt.astype(F32), axis=1, keepdims=True)
    sel = gt | (eq & (_excl_cumsum_lanes(eq.astype(F32), upper) < need))
    pos = _excl_cumsum_lanes(sel.astype(F32), upper)
    pos_scr[...] = jnp.where(sel, pos, -1.0)

    hi, mid, lo = _split3(aff)
    tok = lax.broadcasted_iota(jnp.int32, lg.shape, 0) % N
    w = (hi.astype(F32) + pltpu.roll(mid.astype(F32), E, 1) + pltpu.roll(lo.astype(F32), 2 * E, 1)
         + jnp.where(lane == 3 * E, (tok // TOK_RADIX).astype(F32), 0.0)
         + jnp.where(lane == 3 * E + 1, (tok % TOK_RADIX).astype(F32), 0.0))
    w_scr[...] = w.astype(BF16)

    bn = min(N, 2 * ROUTE_LANE_BLOCK)
    p_iota = lax.broadcasted_iota(jnp.int32, (CAP, bn), 0).astype(F32)
    out_lane = lax.broadcasted_iota(jnp.int32, (CAP, LANES), 1)

    def per_row(i, carry):
        q = i // E
        e = i % E
        prow = pos_scr[pl.ds(i, 1), :]
        acc = jnp.zeros((CAP, LANES), F32)
        for j in range(N // bn):
            onehot = (prow[:, j * bn:(j + 1) * bn] == p_iota).astype(BF16)
            acc = acc + _dot(onehot, w_scr[pl.ds(pl.multiple_of(q * N + j * bn, bn), bn), :])
        gmask = (out_lane == e) | (out_lane == E + e) | (out_lane == 2 * E + e)
        gate = jnp.sum(jnp.where(gmask, acc, 0.0), axis=1, keepdims=True)
        tokid = acc[:, 3 * E:3 * E + 1] * float(TOK_RADIX) + acc[:, 3 * E + 1:3 * E + 2]
        o_ref[q, e] = jnp.where(out_lane == 0, tokid, jnp.where(out_lane == 1, gate, 0.0))
        return carry

    lax.fori_loop(0, RB * E, per_row, 0, unroll=UNROLL)


def _route_requests(logits, n_req, n_tok, cap):
    e = N_EXPERTS
    assert n_tok % ROUTE_LANE_BLOCK == 0 and n_tok <= TOK_RADIX * TOK_RADIX
    rb = max(1, min(n_req, 2048 // n_tok))
    assert n_req % rb == 0
    return pl.pallas_call(
        functools.partial(_route_kernel, N=n_tok, CAP=cap, E=e, RB=rb, UNROLL=4 if rb > 1 else 1),
        out_shape=jax.ShapeDtypeStruct((n_req, e, cap, LANES), F32),
        grid=(n_req // rb,),
        in_specs=[pl.BlockSpec((rb * n_tok, LANES), lambda r: (r, 0))],
        out_specs=pl.BlockSpec((rb, e, cap, LANES), lambda r: (r, 0, 0, 0)),
        scratch_shapes=[pltpu.VMEM((rb * e, n_tok), F32), pltpu.VMEM((rb * n_tok, LANES), BF16)],
        compiler_params=_cparams(("parallel",)),
        name="route",
    )(logits)


def _route(logits, dims):
    e = N_EXPERTS
    bp, bs, seq, dec_seq, p = dims["Bp"], dims["Bs"], dims["seq"], dims["dec_seq"], dims["P"]
    cap_p = CAPACITY_FACTOR * seq // e
    cap_s = CAPACITY_FACTOR * dec_seq // e
    rp = _route_requests(logits[:p], bp, seq, cap_p)
    rs = _route_requests(logits[p:], bs, dec_seq, cap_s)
    ip = rp[..., 0].astype(jnp.int32) + (jnp.arange(bp, dtype=jnp.int32) * seq)[:, None, None]
    is_ = rs[..., 0].astype(jnp.int32) + (p + jnp.arange(bs, dtype=jnp.int32) * dec_seq)[:, None, None]
    idx = jnp.concatenate([jnp.swapaxes(ip, 0, 1).reshape(e, bp * cap_p),
                           jnp.swapaxes(is_, 0, 1).reshape(e, bs * cap_s)], axis=1)
    gate = jnp.concatenate([jnp.swapaxes(rp[..., 1], 0, 1).reshape(e, bp * cap_p),
                            jnp.swapaxes(rs[..., 1], 0, 1).reshape(e, bs * cap_s)], axis=1)
    return idx, gate


def _moe(x, norm_g, mods, layer, w_router_pad, w_gate, w_up, w_down, dims):
    e = N_EXPERTS
    h2, logits = _h2_router(x, norm_g, mods, layer, w_router_pad, dims)
    idx, gate = _route(logits, dims)
    return _moe_fused(x, h2, idx.reshape(-1), gate.reshape(-1, 1), w_gate, w_up, w_down, mods, layer, dims)


def _rope_tables(dims, dh):
    n, p = dims["dec_seq"], dims["P"]
    axis_dim = dh // 2
    tpos = jnp.arange(n)
    rowp = (tpos // GRID_W).astype(F32)
    colp = (tpos % GRID_W).astype(F32)
    inv = ROPE_BASE ** (-jnp.arange(0, axis_dim, 2, dtype=F32) / axis_dim)
    ar = rowp[:, None] * inv
    ac = colp[:, None] * inv
    cos = jnp.concatenate([jnp.cos(ar), jnp.cos(ar), jnp.cos(ac), jnp.cos(ac)], axis=1)
    sin = jnp.concatenate([-jnp.sin(ar), jnp.sin(ar), -jnp.sin(ac), jnp.sin(ac)], axis=1)
    cos = jnp.concatenate([jnp.ones((p, dh), F32), jnp.tile(cos, (dims["Bs"], 1))], axis=0)
    sin = jnp.concatenate([jnp.zeros((p, dh), F32), jnp.tile(sin, (dims["Bs"], 1))], axis=0)
    return cos, sin


def kernel(x_prompt, x_sample, state_mlstm_C, state_mlstm_n, state_mlstm_m, cache_attn_k, cache_attn_v,
           c, c_ctx, w_mod, b_mod, norm1_g, norm2_g, m_w_in, m_b_gate, m_norm_g, m_w_out,
           a_w_in, a_q_norm_g, a_k_norm_g, a_sink, a_w_out, e_w_router, e_w_gate, e_w_up, e_w_down):
    bp, seq, d = x_prompt.shape
    bs, dec_seq, _ = x_sample.shape
    depth = w_mod.shape[0]
    dims = dict(Bp=bp, Bs=bs, seq=seq, dec_seq=dec_seq, P=bp * seq, S=bs * dec_seq, D=d)
    nh = MLSTM_HEADS
    dv = d // nh
    dk = dv // 2
    qk, vw = nh * dk, nh * dv
    dh = d // ATTN_HEADS

    x = jnp.concatenate([x_prompt.reshape(bp * seq, d), x_sample.reshape(bs * dec_seq, d)], axis=0)

    cvecs = jnp.concatenate([c_ctx[None, :], c, jnp.zeros((N_GROUPS_PAD - 1 - bs, d), F32)], axis=0)
    mods = _adaln(cvecs, w_mod, b_mod).reshape(depth, N_GROUPS_PAD, N_MOD, 1, d)
    cos_t, sin_t = _rope_tables(dims, dh)

    new_c, new_n, new_m, new_k, new_v = [], [], [], [], []
    for l in range(depth):
        j = l // 2
        if l % 2 == 0:
            w_in = m_w_in[j]
            w_main = w_in[:, :2 * qk + 2 * vw].astype(BF16)
            w_gate = jnp.pad(w_in[:, 2 * qk + 2 * vw:], ((0, 0), (0, LANES - 4 * nh))).astype(BF16)
            b_gate = jnp.pad(m_b_gate[j], (0, LANES - 4 * nh)).reshape(1, LANES)
            main, gates = _mlstm_inproj(x, norm1_g[l].reshape(1, d), mods, l, w_main, w_gate, b_gate, dims)
            c0 = state_mlstm_C[:, j]
            n0 = state_mlstm_n[:, j].reshape(bs, 2 * nh, dk)
            m0 = jnp.broadcast_to(state_mlstm_m[:, j].reshape(bs, 2 * nh, 1), (bs, 2 * nh, LANES))
            hf, hb, cfin, nfin, mfin = _mlstm_scan(main, gates, c0, n0, m0, dims)
            new_c.append(cfin[:bp])
            new_n.append(nfin[:bp].reshape(bp, 2, nh, dk))
            new_m.append(mfin[:bp, :, 0].reshape(bp, 2, nh))
            x = _mlstm_out(x, hf, hb, main, m_norm_g[j].reshape(1, vw), m_w_out[j].astype(BF16), mods, l, dims)
        else:
            qh, kvh = _attn_inproj(x, norm1_g[l].reshape(1, d), mods, l, a_w_in[j].astype(BF16),
                                   a_q_norm_g[j].reshape(1, dh), a_k_norm_g[j].reshape(1, dh), cos_t, sin_t, dims)
            kvw = ATTN_KV_HEADS * dh
            new_k.append(kvh[:bp * seq, :kvw].reshape(bp, seq, ATTN_KV_HEADS, dh).transpose(0, 2, 1, 3))
            new_v.append(kvh[:bp * seq, kvw:].reshape(bp, seq, ATTN_KV_HEADS, dh).transpose(0, 2, 1, 3))
            o_ctx, o_lat = _attention(qh, kvh, cache_attn_k[:, j], cache_attn_v[:, j], a_sink[j], dims)
            x = _proj_res(x, o_ctx, o_lat, a_w_out[j].astype(BF16), mods, l, 2, dims)
        w_router_pad = jnp.pad(e_w_router[l], ((0, 0), (0, LANES - N_EXPERTS))).astype(BF16)
        x = _moe(x, norm2_g[l].reshape(1, d), mods, l, w_router_pad, e_w_gate, e_w_up, e_w_down, dims)

    p = bp * seq
    return (x[:p].reshape(bp, seq, d), x[p:].reshape(bs, dec_seq, d),
            jnp.stack(new_c, axis=1), jnp.stack(new_n, axis=1), jnp.stack(new_m, axis=1),
            jnp.stack(new_k, axis=1), jnp.stack(new_v, axis=1))
```
